```python
import math
import jax, jax.numpy as jnp
from jax import lax
import numpy as np

D_MODEL = 1024
BATCH = 4
SEQ = 4096
DEPTH = 2
DEC_BATCH = 32
DEC_SEQ = 64
PAST_LEN = 4096

CHUNK = 64
HEAD_DIM = 64
A_HEADS = 8
A_BAND = 8
A_WIDTH = A_HEADS * HEAD_DIM
REL_CLIP = 128
B_HEADS = D_MODEL // (2 * HEAD_DIM)
B_QK_WIDTH = B_HEADS * 2 * HEAD_DIM
B_V_WIDTH = B_HEADS * 2 * HEAD_DIM
IN_WIDTH = 3 * A_WIDTH + 2 * B_QK_WIDTH + B_V_WIDTH
IN_SPLITS = (A_WIDTH, 2 * A_WIDTH, 3 * A_WIDTH, 3 * A_WIDTH + B_QK_WIDTH, 3 * A_WIDTH + 2 * B_QK_WIDTH)
D_FF = 4 * D_MODEL
PLE_DIM = 256
Q_BLOCK = 128
NORM_EPS = 1e-6
MASK_VALUE = -1e30

kernel_name = "streaming_hybrid_chunkband_diffattn"


def rms_norm(x, g):
    x32 = x.astype(jnp.float32)
    y = x32 * lax.rsqrt(jnp.mean(x32 * x32, axis=-1, keepdims=True) + NORM_EPS)
    return (y * g.astype(jnp.float32)).astype(x.dtype)


def alibi_slopes(n):
    return jnp.asarray(2.0 ** (-8.0 * np.arange(1, n + 1) / n), dtype=jnp.float32)


def lambda_init(layer):
    return 0.8 - 0.6 * math.exp(-0.3 * layer)


def band_mask(q_pos, k_pos):
    qc = q_pos[:, :, None] // CHUNK
    kc = k_pos[:, None, :] // CHUNK
    return (k_pos[:, None, :] >= 0) & (kc <= qc) & (kc >= qc - A_BAND)


def band_attention(q, k, v, q_pos, k_pos, rel_table):
    s = jnp.einsum('bnqhd,bnkhd->bnhqk', q, k).astype(jnp.float32) * (HEAD_DIM ** -0.5)
    rel = jnp.clip(q_pos[:, :, None] - k_pos[:, None, :], -REL_CLIP, REL_CLIP) + REL_CLIP
    bias = jnp.moveaxis(rel_table.astype(jnp.float32)[:, rel], 0, 1)
    s = jnp.where(band_mask(q_pos, k_pos)[None, :, None], s + bias[None], MASK_VALUE)
    p = jax.nn.softmax(s, axis=-1).astype(v.dtype)
    return jnp.einsum('bnhqk,bnkhd->bnqhd', p, v)


def mixer_a_prompt(qa, ka, va, rel_table):
    b, s = qa.shape[:2]
    n = s // CHUNK

    def chunks(t):
        return t.reshape(b, n, CHUNK, A_HEADS, HEAD_DIM)

    def band(t):
        tp = jnp.pad(chunks(t), ((0, 0), (A_BAND, 0), (0, 0), (0, 0), (0, 0)))
        return jnp.concatenate([tp[:, j:j + n] for j in range(A_BAND + 1)], axis=2)

    c = jnp.arange(n)[:, None]
    i = jnp.arange(CHUNK)[None, :]
    q_pos = c * CHUNK + i
    k_pos = jnp.concatenate([(c - A_BAND + j) * CHUNK + i for j in range(A_BAND + 1)], axis=1)
    o = band_attention(chunks(qa), band(ka), band(va), q_pos, k_pos, rel_table)
    return o.reshape(b, s, A_WIDTH)


def mixer_a_sample(qa, ka, va, cache_k, cache_v, past, rel_table):
    b, t = qa.shape[:2]
    w = cache_k.shape[1]
    k = jnp.concatenate([cache_k, ka], axis=1)[:, None]
    v = jnp.concatenate([cache_v, va], axis=1)[:, None]
    q_pos = (past + jnp.arange(t))[None]
    k_pos = jnp.concatenate([past - w + jnp.arange(w), past + jnp.arange(t)])[None]
    o = band_attention(qa[:, None], k, v, q_pos, k_pos, rel_table)
    return o.reshape(b, t, A_WIDTH)


def diff_attention(q, k, v, q_pos, k_pos, lam, lam0, sub_g):
    s = jnp.einsum('bqhcd,bkhcd->bhcqk', q, k).astype(jnp.float32) * (HEAD_DIM ** -0.5)
    dist = jnp.abs(q_pos[:, None] - k_pos[None, :]).astype(jnp.float32)
    s = s - alibi_slopes(B_HEADS)[:, None, None, None] * dist
    valid = (k_pos[None, :] // CHUNK) <= (q_pos[:, None] // CHUNK)
    p = jax.nn.softmax(jnp.where(valid, s, MASK_VALUE), axis=-1)
    a = (p[:, :, 0] - lam * p[:, :, 1]).astype(v.dtype)
    o = jnp.einsum('bhqk,bkhe->bqhe', a, v)
    o = rms_norm(o, sub_g) * (1.0 - lam0)
    return o.reshape(o.shape[0], o.shape[1], B_V_WIDTH)


def mixer_b_prompt(qb, kb, vb, lam, lam0, sub_g):
    b, s = qb.shape[:2]
    nb = s // Q_BLOCK
    pos = jnp.arange(s)
    q_blocks = jnp.moveaxis(qb.reshape(b, nb, Q_BLOCK, B_HEADS, 2, HEAD_DIM), 1, 0)
    p_blocks = pos.reshape(nb, Q_BLOCK)
    out = lax.map(lambda a: diff_attention(a[0], kb, vb, a[1], pos, lam, lam0, sub_g), (q_blocks, p_blocks))
    return jnp.moveaxis(out, 0, 1).reshape(b, s, B_V_WIDTH)


def split_projection(u, w):
    b, t = u.shape[:2]
    qa, ka, va, qb, kb, vb = jnp.split(u @ w, IN_SPLITS, axis=-1)
    head_a = (b, t, A_HEADS, HEAD_DIM)
    head_b = (b, t, B_HEADS, 2, HEAD_DIM)
    return (qa.reshape(head_a), ka.reshape(head_a), va.reshape(head_a),
            qb.reshape(head_b), kb.reshape(head_b), vb.reshape(b, t, B_HEADS, 2 * HEAD_DIM))


def setup_inputs(seed: int = 0) -> dict:
    key = jax.random.key(seed)
    ks = jax.random.split(key, 32)
    a_win = min(A_BAND * CHUNK, PAST_LEN)
    nrm = lambda k, shape, scale: jax.random.normal(k, shape, jnp.float32) * scale
    gain = lambda k, shape: 1.0 + 0.01 * jax.random.normal(k, shape, jnp.float32)
    return {
        "x_prompt": nrm(ks[0], (BATCH, SEQ, D_MODEL), 1.0),
        "x_sample": nrm(ks[1], (DEC_BATCH, DEC_SEQ, D_MODEL), 1.0),
        "cache_a_k": nrm(ks[2], (DEPTH, DEC_BATCH, a_win, A_HEADS, HEAD_DIM), 1.0),
        "cache_a_v": nrm(ks[3], (DEPTH, DEC_BATCH, a_win, A_HEADS, HEAD_DIM), 1.0),
        "cache_b_k": nrm(ks[4], (DEPTH, DEC_BATCH, PAST_LEN, B_HEADS, 2, HEAD_DIM), 1.0),
        "cache_b_v": nrm(ks[5], (DEPTH, DEC_BATCH, PAST_LEN, B_HEADS, 2 * HEAD_DIM), 1.0),
        "p_prompt": nrm(ks[6], (DEPTH, BATCH, SEQ, PLE_DIM), 1.0),
        "p_sample": nrm(ks[7], (DEPTH, DEC_BATCH, DEC_SEQ, PLE_DIM), 1.0),
        "w_in": nrm(ks[8], (DEPTH, D_MODEL, IN_WIDTH), D_MODEL ** -0.5),
        "rel_bias": nrm(ks[9], (DEPTH, A_HEADS, 2 * REL_CLIP + 1), 0.1),
        "diff_lambda": nrm(ks[10], (DEPTH, 4, HEAD_DIM), 0.1),
        "diff_subln": gain(ks[11], (DEPTH, 2 * HEAD_DIM)),
        "w_gate": nrm(ks[12], (DEPTH, D_MODEL, 2 * D_MODEL), D_MODEL ** -0.5),
        "w_branch_a": nrm(ks[13], (DEPTH, A_WIDTH, D_MODEL), A_WIDTH ** -0.5),
        "w_branch_b": nrm(ks[14], (DEPTH, B_V_WIDTH, D_MODEL), B_V_WIDTH ** -0.5),
        "w_out": nrm(ks[15], (DEPTH, D_MODEL, D_MODEL), D_MODEL ** -0.5),
        "g_mix_pre": gain(ks[16], (DEPTH, D_MODEL)),
        "g_mix_post": gain(ks[17], (DEPTH, D_MODEL)),
        "g_ffn_pre": gain(ks[18], (DEPTH, D_MODEL)),
        "g_ffn_post": gain(ks[19], (DEPTH, D_MODEL)),
        "w_up": nrm(ks[20], (DEPTH, D_MODEL, D_FF), D_MODEL ** -0.5),
        "w_down": nrm(ks[21], (DEPTH, D_FF, D_MODEL), D_FF ** -0.5),
        "w_ple": nrm(ks[22], (DEPTH, PLE_DIM, D_MODEL), PLE_DIM ** -0.5),
        "g_ple": gain(ks[23], (DEPTH, D_MODEL)),
        "w_ple_gate": nrm(ks[24], (DEPTH, D_MODEL, D_MODEL), D_MODEL ** -0.5),
    }


def reference(x_prompt, x_sample, cache_a_k, cache_a_v, cache_b_k, cache_b_v, p_prompt, p_sample,
              w_in, rel_bias, diff_lambda, diff_subln, w_gate, w_branch_a, w_branch_b, w_out,
              g_mix_pre, g_mix_post, g_ffn_pre, g_ffn_post, w_up, w_down, w_ple, g_ple, w_ple_gate):
    past = cache_b_k.shape[2]

    def layer(i, h, ple, cache):
        u = rms_norm(h, g_mix_pre[i])
        qa, ka, va, qb, kb, vb = split_projection(u, w_in[i])
        lq = diff_lambda[i].astype(jnp.float32)
        lam0 = lambda_init(i)
        lam = jnp.exp(jnp.sum(lq[0] * lq[1])) - jnp.exp(jnp.sum(lq[2] * lq[3])) + lam0
        if cache is None:
            oa = mixer_a_prompt(qa, ka, va, rel_bias[i])
            ob = mixer_b_prompt(qb, kb, vb, lam, lam0, diff_subln[i])
            keep = min(A_BAND * CHUNK, h.shape[1])
            new = (ka[:, h.shape[1] - keep:], va[:, h.shape[1] - keep:], kb, vb)
        else:
            ca_k, ca_v, cb_k, cb_v = cache
            oa = mixer_a_sample(qa, ka, va, ca_k, ca_v, past, rel_bias[i])
            t = h.shape[1]
            ob = diff_attention(qb, jnp.concatenate([cb_k, kb], axis=1), jnp.concatenate([cb_v, vb], axis=1),
                                past + jnp.arange(t), jnp.arange(past + t), lam, lam0, diff_subln[i])
            new = (ka, va, kb, vb)
        g_a, g_b = jnp.split(jax.nn.sigmoid(u @ w_gate[i]), 2, axis=-1)
        mix = (g_a * (oa @ w_branch_a[i]) + g_b * (ob @ w_branch_b[i])) @ w_out[i]
        h = h + rms_norm(mix, g_mix_post[i])
        f = jnp.square(jax.nn.relu(rms_norm(h, g_ffn_pre[i]) @ w_up[i])) @ w_down[i]
        h = h + rms_norm(f, g_ffn_post[i])
        h = h + jax.nn.sigmoid(h @ w_ple_gate[i]) * rms_norm(ple @ w_ple[i], g_ple[i])
        return h, new

    hp, hs = x_prompt, x_sample
    ak_p, av_p, bk_p, bv_p = [], [], [], []
    ak_s, av_s, bk_s, bv_s = [], [], [], []
    for i in range(DEPTH):
        hp, (a1, a2, a3, a4) = layer(i, hp, p_prompt[i], None)
        hs, (s1, s2, s3, s4) = layer(i, hs, p_sample[i], (cache_a_k[i], cache_a_v[i], cache_b_k[i], cache_b_v[i]))
        ak_p.append(a1); av_p.append(a2); bk_p.append(a3); bv_p.append(a4)
        ak_s.append(s1); av_s.append(s2); bk_s.append(s3); bv_s.append(s4)
    return (hp, hs, jnp.stack(ak_p), jnp.stack(av_p), jnp.stack(bk_p), jnp.stack(bv_p),
            jnp.stack(ak_s), jnp.stack(av_s), jnp.stack(bk_s), jnp.stack(bv_s))
```

```python
import functools
import math

import jax
import jax.numpy as jnp
import numpy as np
from jax import lax
from jax.experimental import pallas as pl
from jax.experimental.pallas import tpu as pltpu

CHUNK = 64
HEAD_DIM = 64
A_HEADS = 8
A_BAND = 8
A_WIDTH = A_HEADS * HEAD_DIM
A_WIN = A_BAND * CHUNK
A_KEYS = A_WIN + CHUNK
REL_CLIP = 128
B_HEADS = 8
B_HEAD_WIDTH = 2 * HEAD_DIM
B_WIDTH = B_HEADS * B_HEAD_WIDTH
NORM_EPS = 1e-6
MASK_VALUE = -1e30
QK_SCALE = HEAD_DIM ** -0.5

A_GROUP = 4
A_GROUP_WIDTH = A_GROUP * HEAD_DIM

V7X_VMEM_LIMIT_BYTES = 60000 * 1024

BF16 = jnp.bfloat16
F32 = jnp.float32


def _lambda_init(layer):
    return 0.8 - 0.6 * math.exp(-0.3 * layer)


def _alibi_slopes(n):
    return jnp.asarray(2.0 ** (-8.0 * np.arange(1, n + 1) / n), dtype=F32)


def _rms(x, g):
    return x * lax.rsqrt(jnp.mean(x * x, axis=-1, keepdims=True) + NORM_EPS) * g


def _dot(a, b):
    return jnp.dot(a, b, preferred_element_type=F32)


def _dot_nt(a, b):
    return lax.dot_general(a, b, (((1,), (1,)), ((), ())), preferred_element_type=F32)


def _resident(shape):
    nd = len(shape)
    return pl.BlockSpec(shape, lambda *_: (0,) * nd, pipeline_mode=pl.Buffered(1))


def _params(*sem):
    return pltpu.CompilerParams(dimension_semantics=sem, vmem_limit_bytes=V7X_VMEM_LIMIT_BYTES)


def _proj_kernel(h_ref, g_ref, win_ref, wg_ref,
                 qa_ref, kaf_ref, vaf_ref, kab_ref, vab_ref,
                 qb_ref, kbf_ref, vbf_ref, kbb_ref, vbb_ref, gate_ref):
    u = _rms(h_ref[...], g_ref[...]).astype(BF16)

    def col(lo, hi):
        return _dot(u, win_ref[:, lo:hi])

    o = 0
    qa_ref[...] = (col(o, o + A_WIDTH) * QK_SCALE).astype(BF16)
    o += A_WIDTH
    ka = col(o, o + A_WIDTH)
    kaf_ref[...] = ka
    kab_ref[...] = ka.astype(BF16)
    o += A_WIDTH
    va = col(o, o + A_WIDTH)
    vaf_ref[...] = va
    vab_ref[...] = va.astype(BF16)
    o += A_WIDTH
    qb_ref[...] = (col(o, o + B_WIDTH) * QK_SCALE).astype(BF16)
    o += B_WIDTH
    kb = col(o, o + B_WIDTH)
    kbf_ref[...] = kb
    kbb_ref[...] = kb.astype(BF16)
    o += B_WIDTH
    vb = col(o, o + B_WIDTH)
    vbf_ref[...] = vb
    vbb_ref[...] = vb.astype(BF16)
    gate_ref[...] = jax.nn.sigmoid(_dot(u, wg_ref[...])).astype(BF16)


def _proj(h, g, w_in, w_gate, tm):
    n, d = h.shape
    widths = [(A_WIDTH, BF16), (A_WIDTH, F32), (A_WIDTH, F32), (A_WIDTH, BF16), (A_WIDTH, BF16),
              (B_WIDTH, BF16), (B_WIDTH, F32), (B_WIDTH, F32), (B_WIDTH, BF16), (B_WIDTH, BF16),
              (w_gate.shape[1], BF16)]
    row = lambda w: pl.BlockSpec((tm, w), lambda i: (i, 0))
    return pl.pallas_call(
        _proj_kernel,
        grid=(n // tm,),
        in_specs=[row(d), _resident((1, d)), _resident(w_in.shape), _resident(w_gate.shape)],
        out_specs=[row(w) for w, _ in widths],
        out_shape=[jax.ShapeDtypeStruct((n, w), dt) for w, dt in widths],
        compiler_params=_params("arbitrary"),
        name="proj",
    )(h, g.reshape(1, d), w_in, w_gate)


def _band_bias_kernel(tab_ref, o_ref):
    rows = A_GROUP * CHUNK
    qi = lax.broadcasted_iota(jnp.int32, (CHUNK, A_KEYS), 0)
    ki = lax.broadcasted_iota(jnp.int32, (CHUNK, A_KEYS), 1)
    idx = jnp.clip(A_WIN + qi - ki, -REL_CLIP, REL_CLIP) + REL_CLIP
    lo = REL_CLIP - CHUNK + 1
    hi = 2 * REL_CLIP

    for h in range(A_HEADS):
        g, hh = divmod(h, A_GROUP)
        o_ref[g, hh * CHUNK:(hh + 1) * CHUNK, :] = jnp.full((CHUNK, A_KEYS), tab_ref[h, hi], F32)

    def body(r, carry):
        hit = idx == r
        for h in range(A_HEADS):
            g, hh = divmod(h, A_GROUP)
            sl = (g, slice(hh * CHUNK, (hh + 1) * CHUNK), slice(None))
            o_ref[sl] = jnp.where(hit, tab_ref[h, r], o_ref[sl])
        return carry

    lax.fori_loop(lo, hi, body, 0)
    del rows


def _band_bias(table):
    return pl.pallas_call(
        _band_bias_kernel,
        in_specs=[pl.BlockSpec(memory_space=pltpu.SMEM)],
        out_specs=pl.BlockSpec(memory_space=pltpu.VMEM),
        out_shape=jax.ShapeDtypeStruct((A_HEADS // A_GROUP, A_GROUP * CHUNK, A_KEYS), F32),
        name="band_bias",
    )(table)


def _band_core(q, kw, vw, bias_ref, first_valid_key):
    lane_head = lax.broadcasted_iota(jnp.int32, (CHUNK, A_GROUP_WIDTH), 1) // HEAD_DIM
    outs = []
    for g in range(A_HEADS // A_GROUP):
        cols = slice(g * A_GROUP_WIDTH, (g + 1) * A_GROUP_WIDTH)
        qg = q[:, cols]
        qbd = jnp.concatenate(
            [jnp.where(lane_head == hh, qg, jnp.zeros_like(qg)) for hh in range(A_GROUP)], axis=0)
        s = _dot_nt(qbd, kw[:, cols]) + bias_ref[g]
        if first_valid_key is not None:
            kidx = lax.broadcasted_iota(jnp.int32, s.shape, 1)
            s = jnp.where(kidx >= first_valid_key, s, MASK_VALUE)
        m = jnp.max(s, axis=-1, keepdims=True)
        p = jnp.exp(s - m)
        l = jnp.sum(p, axis=-1, keepdims=True)
        o = _dot(p.astype(BF16), vw[:, cols]) / l
        og = jnp.zeros((CHUNK, A_GROUP_WIDTH), F32)
        for hh in range(A_GROUP):
            og = og + jnp.where(lane_head == hh, o[hh * CHUNK:(hh + 1) * CHUNK, :], 0.0)
        outs.append(og)
    return jnp.concatenate(outs, axis=1)


def _band_prompt_kernel(q_ref, k_ref, v_ref, bias_ref, o_ref, kpad_ref, vpad_ref):
    c = pl.program_id(1)
    t = k_ref.shape[0]

    @pl.when(c == 0)
    def _():
        kpad_ref[0:A_WIN, :] = jnp.zeros((A_WIN, A_WIDTH), BF16)
        vpad_ref[0:A_WIN, :] = jnp.zeros((A_WIN, A_WIDTH), BF16)
        kpad_ref[A_WIN:A_WIN + t, :] = k_ref[...]
        vpad_ref[A_WIN:A_WIN + t, :] = v_ref[...]

    start = pl.multiple_of(c * CHUNK, CHUNK)
    kw = kpad_ref[pl.ds(start, A_KEYS), :]
    vw = vpad_ref[pl.ds(start, A_KEYS), :]
    first_valid = (A_BAND - c) * CHUNK
    o_ref[...] = _band_core(q_ref[...], kw, vw, bias_ref, first_valid).astype(BF16)


def _band_prompt(qa, ka, va, bias):
    b, t, _ = qa.shape
    return pl.pallas_call(
        _band_prompt_kernel,
        grid=(b, t // CHUNK),
        in_specs=[pl.BlockSpec((None, CHUNK, A_WIDTH), lambda i, c: (i, c, 0)),
                  pl.BlockSpec((None, t, A_WIDTH), lambda i, c: (i, 0, 0)),
                  pl.BlockSpec((None, t, A_WIDTH), lambda i, c: (i, 0, 0)),
                  _resident(bias.shape)],
        out_specs=pl.BlockSpec((None, CHUNK, A_WIDTH), lambda i, c: (i, c, 0)),
        out_shape=jax.ShapeDtypeStruct((b, t, A_WIDTH), BF16),
        scratch_shapes=[pltpu.VMEM((A_WIN + t, A_WIDTH), BF16),
                        pltpu.VMEM((A_WIN + t, A_WIDTH), BF16)],
        compiler_params=_params("arbitrary", "arbitrary"),
        name="band_prompt",
    )(qa, ka, va, bias)


def _band_sample_kernel(q_ref, kc_ref, vc_ref, kn_ref, vn_ref, bias_ref, o_ref):
    kw = jnp.concatenate([kc_ref[...].astype(BF16), kn_ref[...]], axis=0)
    vw = jnp.concatenate([vc_ref[...].astype(BF16), vn_ref[...]], axis=0)
    o_ref[...] = _band_core(q_ref[...], kw, vw, bias_ref, None).astype(BF16)


def _band_sample(qa, ka, va, cache_k, cache_v, bias):
    b, t, _ = qa.shape
    w = cache_k.shape[1]
    new = pl.BlockSpec((None, t, A_WIDTH), lambda i: (i, 0, 0))
    old = pl.BlockSpec((None, w, A_WIDTH), lambda i: (i, 0, 0))
    return pl.pallas_call(
        _band_sample_kernel,
        grid=(b,),
        in_specs=[new, old, old, new, new, _resident(bias.shape)],
        out_specs=new,
        out_shape=jax.ShapeDtypeStruct((b, t, A_WIDTH), BF16),
        compiler_params=_params("arbitrary"),
        name="band_sample",
    )(qa, cache_k, cache_v, ka, va, bias)


def _diff_lambda(lq_ref, lam0):
    lq = lq_ref[...]
    a = jnp.sum(lq[0:1, :] * lq[1:2, :], axis=-1, keepdims=True)
    b = jnp.sum(lq[2:3, :] * lq[3:4, :], axis=-1, keepdims=True)
    return jnp.exp(a) - jnp.exp(b) + lam0


def _stack_components(q):
    lane = lax.broadcasted_iota(jnp.int32, q.shape, 1)
    zero = jnp.zeros_like(q)
    return jnp.concatenate([jnp.where(lane < HEAD_DIM, q, zero),
                            jnp.where(lane >= HEAD_DIM, q, zero)], axis=0)


def _local_bias_and_mask(slope, tq, tk, key_off):
    r = lax.broadcasted_iota(jnp.int32, (2 * tq, tk), 0) % tq
    c = lax.broadcasted_iota(jnp.int32, (2 * tq, tk), 1) + key_off
    bias = slope * jnp.minimum(c, 2 * r - c).astype(F32)
    valid = (c // CHUNK) <= (r // CHUNK)
    return bias, valid


def _diff_finish(acc, l, lam, lam0, subg, t):
    o = acc[:t] / l[:t] - lam * (acc[t:] / l[t:])
    return _rms(o, subg) * (1.0 - lam0)


def _diff_prompt_kernel(lam0, tq, tk, slopes_ref, lq_ref, subg_ref, q_ref, k_ref, v_ref, o_ref,
                        m_ref, l_ref, acc_ref):
    h = pl.program_id(1)
    qi = pl.program_id(2)
    slope = slopes_ref[h]
    qs = _stack_components(q_ref[...])
    q_start = qi * tq

    m_ref[...] = jnp.full(m_ref.shape, MASK_VALUE, F32)
    l_ref[...] = jnp.zeros(l_ref.shape, F32)
    acc_ref[...] = jnp.zeros(acc_ref.shape, F32)

    def update(s, vblk):
        m_old = m_ref[...]
        m_new = jnp.maximum(m_old, jnp.max(s, axis=-1, keepdims=True))
        alpha = jnp.exp(m_old - m_new)
        p = jnp.exp(s - m_new)
        l_ref[...] = alpha * l_ref[...] + jnp.sum(p, axis=-1, keepdims=True)
        acc_ref[...] = alpha * acc_ref[...] + _dot(p.astype(BF16), vblk)
        m_ref[...] = m_new

    def below(j, carry):
        ks = pl.multiple_of(j * tk, tk)
        kpos = ks - q_start + lax.broadcasted_iota(jnp.int32, (1, tk), 1)
        s = _dot_nt(qs, k_ref[pl.ds(ks, tk), :]) + slope * kpos.astype(F32)
        update(s, v_ref[pl.ds(ks, tk), :])
        return carry

    lax.fori_loop(0, q_start // tk, below, 0)

    for d in range(tq // tk):
        ks = pl.multiple_of(q_start + d * tk, tk)
        bias, valid = _local_bias_and_mask(slope, tq, tk, d * tk)
        s = _dot_nt(qs, k_ref[pl.ds(ks, tk), :]) + bias
        update(jnp.where(valid, s, MASK_VALUE), v_ref[pl.ds(ks, tk), :])

    lam = _diff_lambda(lq_ref, lam0)
    o_ref[...] = _diff_finish(acc_ref[...], l_ref[...], lam, lam0, subg_ref[...], tq).astype(BF16)


def _diff_prompt(qb, kb, vb, lq, subg, lam0, tq, tk):
    b, t, _ = qb.shape
    blk = pl.BlockSpec((None, tq, B_HEAD_WIDTH), lambda i, h, q: (i, q, h))
    full = pl.BlockSpec((None, t, B_HEAD_WIDTH), lambda i, h, q: (i, 0, h))
    return pl.pallas_call(
        functools.partial(_diff_prompt_kernel, lam0, tq, tk),
        grid=(b, B_HEADS, t // tq),
        in_specs=[pl.BlockSpec(memory_space=pltpu.SMEM), _resident(lq.shape),
                  _resident((1, B_HEAD_WIDTH)), blk, full, full],
        out_specs=blk,
        out_shape=jax.ShapeDtypeStruct((b, t, B_WIDTH), BF16),
        scratch_shapes=[pltpu.VMEM((2 * tq, 1), F32), pltpu.VMEM((2 * tq, 1), F32),
                        pltpu.VMEM((2 * tq, B_HEAD_WIDTH), F32)],
        compiler_params=_params("arbitrary", "arbitrary", "arbitrary"),
        name="diff_prompt",
    )(_alibi_slopes(B_HEADS), lq, subg.reshape(1, B_HEAD_WIDTH), qb, kb, vb)


def _diff_sample_kernel(lam0, hb, slopes_ref, lq_ref, subg_ref, q_ref, kc_ref, vc_ref, kn_ref, vn_ref,
                        o_ref):
    hg = pl.program_id(1)
    t = q_ref.shape[0]
    past = kc_ref.shape[0]
    lam = _diff_lambda(lq_ref, lam0)
    kpos = (lax.broadcasted_iota(jnp.int32, (1, past), 1) - past).astype(F32)
    for hh in range(hb):
        cols = slice(hh * B_HEAD_WIDTH, (hh + 1) * B_HEAD_WIDTH)
        slope = slopes_ref[hg * hb + hh]
        qs = _stack_components(q_ref[:, cols])
        s_old = _dot_nt(qs, kc_ref[:, cols].astype(BF16)) + slope * kpos
        bias, _ = _local_bias_and_mask(slope, t, t, 0)
        s_new = _dot_nt(qs, kn_ref[:, cols]) + bias
        m = jnp.maximum(jnp.max(s_old, axis=-1, keepdims=True), jnp.max(s_new, axis=-1, keepdims=True))
        p_old = jnp.exp(s_old - m)
        p_new = jnp.exp(s_new - m)
        l = jnp.sum(p_old, axis=-1, keepdims=True) + jnp.sum(p_new, axis=-1, keepdims=True)
        acc = (_dot(p_old.astype(BF16), vc_ref[:, cols].astype(BF16))
               + _dot(p_new.astype(BF16), vn_ref[:, cols]))
        o_ref[:, cols] = _diff_finish(acc, l, lam, lam0, subg_ref[...], t).astype(BF16)


def _diff_sample(qb, kb, vb, cache_k, cache_v, lq, subg, lam0, hb):
    b, t, _ = qb.shape
    past = cache_k.shape[1]
    new = pl.BlockSpec((None, t, hb * B_HEAD_WIDTH), lambda i, h: (i, 0, h))
    old = pl.BlockSpec((None, past, hb * B_HEAD_WIDTH), lambda i, h: (i, 0, h))
    return pl.pallas_call(
        functools.partial(_diff_sample_kernel, lam0, hb),
        grid=(b, B_HEADS // hb),
        in_specs=[pl.BlockSpec(memory_space=pltpu.SMEM), _resident(lq.shape),
                  _resident((1, B_HEAD_WIDTH)), new, old, old, new, new],
        out_specs=new,
        out_shape=jax.ShapeDtypeStruct((b, t, B_WIDTH), BF16),
        compiler_params=_params("arbitrary", "arbitrary"),
        name="diff_sample",
    )(_alibi_slopes(B_HEADS), lq, subg.reshape(1, B_HEAD_WIDTH), qb, cache_k, cache_v, kb, vb)


def _merge_kernel(ff_chunk, h_ref, gate_ref, oa_ref, ob_ref, ple_ref,
                  wba_ref, wbb_ref, wout_ref, wup_ref, wdown_ref, wple_ref, wpg_ref,
                  g_post_ref, g_fpre_ref, g_fpost_ref, g_ple_ref, o_ref):
    d = h_ref.shape[1]
    gate = gate_ref[...]
    mix = (gate[:, :d].astype(F32) * _dot(oa_ref[...], wba_ref[...])
           + gate[:, d:].astype(F32) * _dot(ob_ref[...], wbb_ref[...]))
    h = h_ref[...] + _rms(_dot(mix.astype(BF16), wout_ref[...]), g_post_ref[...])

    u = _rms(h, g_fpre_ref[...]).astype(BF16)
    f = jnp.zeros(h.shape, F32)
    for lo in range(0, wup_ref.shape[1], ff_chunk):
        act = jnp.square(jnp.maximum(_dot(u, wup_ref[:, lo:lo + ff_chunk]), 0.0))
        f = f + _dot(act.astype(BF16), wdown_ref[lo:lo + ff_chunk, :])
    h = h + _rms(f, g_fpost_ref[...])

    pg = jax.nn.sigmoid(_dot(h.astype(BF16), wpg_ref[...]))
    pe = _rms(_dot(ple_ref[...].astype(BF16), wple_ref[...]), g_ple_ref[...])
    o_ref[...] = h + pg * pe


def _merge(h, gate, oa, ob, ple, w, gains, tm, ff_chunk):
    n, d = h.shape
    row = lambda a: pl.BlockSpec((tm, a.shape[1]), lambda i: (i, 0))
    acts = [h, gate, oa, ob, ple]
    gains = [g.reshape(1, d) for g in gains]
    return pl.pallas_call(
        functools.partial(_merge_kernel, ff_chunk),
        grid=(n // tm,),
        in_specs=[row(a) for a in acts] + [_resident(x.shape) for x in w] + [_resident((1, d))] * 4,
        out_specs=pl.BlockSpec((tm, d), lambda i: (i, 0)),
        out_shape=jax.ShapeDtypeStruct((n, d), F32),
        compiler_params=_params("arbitrary"),
        name="merge",
    )(*acts, *w, *gains)


ROW_TILE = 256
FF_CHUNK = 1024
DIFF_TQ = 256
DIFF_TK = 256
DIFF_SAMPLE_HEADS = 2


def kernel(x_prompt, x_sample, cache_a_k, cache_a_v, cache_b_k, cache_b_v, p_prompt, p_sample,
           w_in, rel_bias, diff_lambda, diff_subln, w_gate, w_branch_a, w_branch_b, w_out,
           g_mix_pre, g_mix_post, g_ffn_pre, g_ffn_post, w_up, w_down, w_ple, g_ple, w_ple_gate):
    depth = w_in.shape[0]
    bp, tp, d = x_prompt.shape
    bs, ts, _ = x_sample.shape
    past = cache_b_k.shape[2]
    a_win = cache_a_k.shape[2]
    assert a_win == A_WIN and past >= A_WIN and ts == CHUNK and tp % DIFF_TQ == 0

    wb = [x.astype(BF16) for x in (w_in, w_gate, w_branch_a, w_branch_b, w_out, w_up, w_down,
                                   w_ple, w_ple_gate)]
    w_in_b, w_gate_b, w_ba, w_bb, w_out_b, w_up_b, w_down_b, w_ple_b, w_pg_b = wb

    hp = x_prompt.reshape(bp * tp, d)
    hs = x_sample.reshape(bs * ts, d)
    outs = [[] for _ in range(8)]
    keep = min(A_WIN, tp)

    for i in range(depth):
        lam0 = _lambda_init(i)
        bias = _band_bias(rel_bias[i])
        merge_w = (w_ba[i], w_bb[i], w_out_b[i], w_up_b[i], w_down_b[i], w_ple_b[i], w_pg_b[i])
        gains = (g_mix_post[i], g_ffn_pre[i], g_ffn_post[i], g_ple[i])

        qa, kaf, vaf, kab, vab, qb, kbf, vbf, kbb, vbb, gate = _proj(
            hp, g_mix_pre[i], w_in_b[i], w_gate_b[i], ROW_TILE)
        r3 = lambda a: a.reshape(bp, tp, a.shape[-1])
        oa = _band_prompt(r3(qa), r3(kab), r3(vab), bias)
        ob = _diff_prompt(r3(qb), r3(kbb), r3(vbb), diff_lambda[i], diff_subln[i], lam0,
                          DIFF_TQ, DIFF_TK)
        hp = _merge(hp, gate, oa.reshape(bp * tp, -1), ob.reshape(bp * tp, -1),
                    p_prompt[i].reshape(bp * tp, -1), merge_w, gains, ROW_TILE, FF_CHUNK)
        outs[0].append(r3(kaf)[:, tp - keep:].reshape(bp, keep, A_HEADS, HEAD_DIM))
        outs[1].append(r3(vaf)[:, tp - keep:].reshape(bp, keep, A_HEADS, HEAD_DIM))
        outs[2].append(kbf.reshape(bp, tp, B_HEADS, 2, HEAD_DIM))
        outs[3].append(vbf.reshape(bp, tp, B_HEADS, B_HEAD_WIDTH))

        qa, kaf, vaf, kab, vab, qb, kbf, vbf, kbb, vbb, gate = _proj(
            hs, g_mix_pre[i], w_in_b[i], w_gate_b[i], ROW_TILE)
        r3 = lambda a: a.reshape(bs, ts, a.shape[-1])
        oa = _band_sample(r3(qa), r3(kab), r3(vab),
                          cache_a_k[i].reshape(bs, a_win, A_WIDTH),
                          cache_a_v[i].reshape(bs, a_win, A_WIDTH), bias)
        ob = _diff_sample(r3(qb), r3(kbb), r3(vbb),
                          cache_b_k[i].reshape(bs, past, B_WIDTH),
                          cache_b_v[i].reshape(bs, past, B_WIDTH),
                          diff_lambda[i], diff_subln[i], lam0, DIFF_SAMPLE_HEADS)
        hs = _merge(hs, gate, oa.reshape(bs * ts, -1), ob.reshape(bs * ts, -1),
                    p_sample[i].reshape(bs * ts, -1), merge_w, gains, ROW_TILE, FF_CHUNK)
        outs[4].append(kaf.reshape(bs, ts, A_HEADS, HEAD_DIM))
        outs[5].append(vaf.reshape(bs, ts, A_HEADS, HEAD_DIM))
        outs[6].append(kbf.reshape(bs, ts, B_HEADS, 2, HEAD_DIM))
        outs[7].append(vbf.reshape(bs, ts, B_HEADS, B_HEAD_WIDTH))

    return (hp.reshape(bp, tp, d), hs.reshape(bs, ts, d)) + tuple(jnp.stack(o) for o in outs)
```

```python
import functools
import math

import jax
import jax.numpy as jnp
import numpy as np
from jax import lax
from jax.experimental import pallas as pl
from jax.experimental.pallas import tpu as pltpu

CHUNK = 64
HEAD_DIM = 64
A_HEADS = 8
A_BAND = 8
A_WIDTH = A_HEADS * HEAD_DIM
A_WIN = A_BAND * CHUNK
A_KEYS = A_WIN + CHUNK
REL_CLIP = 128
B_HEADS = 8
B_HEAD_WIDTH = 2 * HEAD_DIM
B_WIDTH = B_HEADS * B_HEAD_WIDTH
NORM_EPS = 1e-6
MASK_VALUE = -1e30
QK_SCALE = HEAD_DIM ** -0.5

A_GROUP = 4
A_GROUP_WIDTH = A_GROUP * HEAD_DIM
LANES = 128

V7X_VMEM_LIMIT_BYTES = 60000 * 1024

BF16 = jnp.bfloat16
F32 = jnp.float32


def _lambda_init(layer):
    return 0.8 - 0.6 * math.exp(-0.3 * layer)


def _alibi_slopes(n):
    return jnp.asarray(2.0 ** (-8.0 * np.arange(1, n + 1) / n), dtype=F32)


def _rms(x, g):
    return x * lax.rsqrt(jnp.mean(x * x, axis=-1, keepdims=True) + NORM_EPS) * g


def _dot(a, b):
    return jnp.dot(a, b, preferred_element_type=F32)


def _dot_nt(a, b):
    return lax.dot_general(a, b, (((1,), (1,)), ((), ())), preferred_element_type=F32)


def _resident(shape):
    nd = len(shape)
    return pl.BlockSpec(shape, lambda *_: (0,) * nd, pipeline_mode=pl.Buffered(1))


def _layer(shape, layer):
    nd = len(shape)
    return pl.BlockSpec((None,) + tuple(shape), lambda *_: (layer,) + (0,) * nd,
                        pipeline_mode=pl.Buffered(1))


def _params(*sem):
    return pltpu.CompilerParams(dimension_semantics=sem, vmem_limit_bytes=V7X_VMEM_LIMIT_BYTES)


IN_QA, IN_KA, IN_VA = 0, A_WIDTH, 2 * A_WIDTH
IN_QB, IN_KB, IN_VB = 3 * A_WIDTH, 3 * A_WIDTH + B_WIDTH, 3 * A_WIDTH + 2 * B_WIDTH


def _store_head_tiled(o_ref, x):
    rows = x.shape[0]
    for h in range(B_HEADS):
        o_ref[pl.ds(h, rows, stride=B_HEADS), :] = x[:, h * B_HEAD_WIDTH:(h + 1) * B_HEAD_WIDTH]


def _proj_kernel(prompt, h_ref, g_ref, win_ref, wg_ref, *out_refs):
    u = _rms(h_ref[...], g_ref[...]).astype(BF16)

    def col(lo, width):
        return _dot(u, win_ref[:, lo:lo + width])

    qa = (col(IN_QA, A_WIDTH) * QK_SCALE).astype(BF16)
    ka = col(IN_KA, A_WIDTH)
    va = col(IN_VA, A_WIDTH)
    qb = (col(IN_QB, B_WIDTH) * QK_SCALE).astype(BF16)
    kb = col(IN_KB, B_WIDTH)
    vb = col(IN_VB, B_WIDTH)
    gate = jax.nn.sigmoid(_dot(u, wg_ref[...])).astype(BF16)
    if prompt:
        qa_ref, kab_ref, vab_ref, qb_ref, kbb_ref, kbt_ref, vbt_ref, vbh_ref, gate_ref = out_refs
        kbt_ref[...] = kb.T
        vbt_ref[...] = vb.T.astype(BF16)
    else:
        (qa_ref, kab_ref, vab_ref, qb_ref, kbb_ref, kaf_ref, vaf_ref, kbf_ref, vbb_ref, vbh_ref,
         gate_ref) = out_refs
        kaf_ref[...] = ka
        vaf_ref[...] = va
        kbf_ref[...] = kb
        vbb_ref[...] = vb.astype(BF16)
    qa_ref[...] = qa
    kab_ref[...] = ka.astype(BF16)
    vab_ref[...] = va.astype(BF16)
    qb_ref[...] = qb
    kbb_ref[...] = kb.astype(BF16)
    _store_head_tiled(vbh_ref, vb)
    gate_ref[...] = gate


def _proj(h, g, w_in, w_gate, layer, tm, prompt_seq):
    n, d = h.shape
    gate_w = w_gate.shape[2]
    row = lambda w: pl.BlockSpec((tm, w), lambda i: (i, 0))
    sds = jax.ShapeDtypeStruct
    tok = lambda w, dt: (row(w), sds((n, w), dt))
    head_tiled = (pl.BlockSpec((tm * B_HEADS, B_HEAD_WIDTH), lambda i: (i, 0)),
                  sds((n * B_HEADS, B_HEAD_WIDTH), F32))
    common = [tok(A_WIDTH, BF16), tok(A_WIDTH, BF16), tok(A_WIDTH, BF16), tok(B_WIDTH, BF16),
              tok(B_WIDTH, BF16)]
    if prompt_seq is not None:
        nblk = prompt_seq // tm
        b = n // prompt_seq
        outs = common + [
            (pl.BlockSpec((None, B_WIDTH, tm), lambda i: (i // nblk, 0, i % nblk)),
             sds((b, B_WIDTH, prompt_seq), F32)),
            (pl.BlockSpec((None, None, B_WIDTH, tm), lambda i: (i // nblk, i % nblk, 0, 0)),
             sds((b, nblk, B_WIDTH, tm), BF16)),
            head_tiled, tok(gate_w, BF16)]
    else:
        outs = common + [tok(A_WIDTH, F32), tok(A_WIDTH, F32), tok(B_WIDTH, F32), tok(B_WIDTH, BF16),
                         head_tiled, tok(gate_w, BF16)]
    return pl.pallas_call(
        functools.partial(_proj_kernel, prompt_seq is not None),
        grid=(n // tm,),
        in_specs=[row(d), _layer((1, d), layer), _layer(w_in.shape[1:], layer),
                  _layer(w_gate.shape[1:], layer)],
        out_specs=[s for s, _ in outs],
        out_shape=[s for _, s in outs],
        compiler_params=_params("arbitrary"),
        name="proj",
    )(h, g.reshape(g.shape[0], 1, d), w_in, w_gate)


def _proj_tail_kernel(h_ref, g_ref, wk_ref, wv_ref, kt_ref, vt_ref):
    u = _rms(h_ref[...], g_ref[...]).astype(BF16)
    kt_ref[...] = _dot(u, wk_ref[...]).T
    vt_ref[...] = _dot(u, wv_ref[...]).T


def _proj_tail(h3, g, w_in, layer, keep):
    b, t, d = h3.shape
    last = t // keep - 1
    wcol = lambda c: pl.BlockSpec((None, d, A_WIDTH), lambda i: (layer, 0, c), pipeline_mode=pl.Buffered(1))
    out = pl.BlockSpec((None, A_WIDTH, keep), lambda i: (i, 0, 0))
    return pl.pallas_call(
        _proj_tail_kernel,
        grid=(b,),
        in_specs=[pl.BlockSpec((None, keep, d), lambda i: (i, last, 0)), _layer((1, d), layer),
                  wcol(IN_KA // A_WIDTH), wcol(IN_VA // A_WIDTH)],
        out_specs=[out, out],
        out_shape=[jax.ShapeDtypeStruct((b, A_WIDTH, keep), F32)] * 2,
        compiler_params=_params("arbitrary"),
        name="proj_tail",
    )(h3, g.reshape(g.shape[0], 1, d), w_in, w_in)


def _band_bias_kernel(tab_ref, o_ref):
    qi = lax.broadcasted_iota(jnp.int32, (CHUNK, A_KEYS), 0)
    ki = lax.broadcasted_iota(jnp.int32, (CHUNK, A_KEYS), 1)
    idx = jnp.clip(A_WIN + qi - ki, -REL_CLIP, REL_CLIP) + REL_CLIP
    lo = REL_CLIP - CHUNK + 1
    hi = 2 * REL_CLIP

    for h in range(A_HEADS):
        g, hh = divmod(h, A_GROUP)
        o_ref[g, hh * CHUNK:(hh + 1) * CHUNK, :] = jnp.full((CHUNK, A_KEYS), tab_ref[h, hi], F32)

    def body(r, carry):
        hit = idx == r
        for h in range(A_HEADS):
            g, hh = divmod(h, A_GROUP)
            sl = (g, slice(hh * CHUNK, (hh + 1) * CHUNK), slice(None))
            o_ref[sl] = jnp.where(hit, tab_ref[h, r], o_ref[sl])
        return carry

    lax.fori_loop(lo, hi, body, 0)


def _band_bias(table):
    return pl.pallas_call(
        _band_bias_kernel,
        in_specs=[pl.BlockSpec(memory_space=pltpu.SMEM)],
        out_specs=pl.BlockSpec(memory_space=pltpu.VMEM),
        out_shape=jax.ShapeDtypeStruct((A_HEADS // A_GROUP, A_GROUP * CHUNK, A_KEYS), F32),
        name="band_bias",
    )(table)


def _block_diag_query(qg):
    lane_head = lax.broadcasted_iota(jnp.int32, qg.shape, 1) // HEAD_DIM
    return jnp.concatenate(
        [jnp.where(lane_head == hh, qg, jnp.zeros_like(qg)) for hh in range(A_GROUP)], axis=0)


def _gather_heads(o):
    lane_head = lax.broadcasted_iota(jnp.int32, (CHUNK, A_GROUP_WIDTH), 1) // HEAD_DIM
    og = jnp.zeros((CHUNK, A_GROUP_WIDTH), F32)
    for hh in range(A_GROUP):
        og = og + jnp.where(lane_head == hh, o[hh * CHUNK:(hh + 1) * CHUNK, :], 0.0)
    return og


def _band_prompt_kernel(q_ref, k_ref, v_ref, bias_ref, o_ref, kpad_ref, vpad_ref):
    c = pl.program_id(1)
    t = k_ref.shape[0]

    @pl.when(c == 0)
    def _():
        kpad_ref[0:A_WIN, :] = jnp.zeros((A_WIN, A_WIDTH), BF16)
        vpad_ref[0:A_WIN, :] = jnp.zeros((A_WIN, A_WIDTH), BF16)
        kpad_ref[A_WIN:A_WIN + t, :] = k_ref[...]
        vpad_ref[A_WIN:A_WIN + t, :] = v_ref[...]

    start = pl.multiple_of(c * CHUNK, CHUNK)
    kw = kpad_ref[pl.ds(start, A_KEYS), :]
    vw = vpad_ref[pl.ds(start, A_KEYS), :]
    first_valid = (A_BAND - c) * CHUNK
    q = q_ref[...]
    outs = []
    for g in range(A_HEADS // A_GROUP):
        cols = slice(g * A_GROUP_WIDTH, (g + 1) * A_GROUP_WIDTH)
        s = _dot_nt(_block_diag_query(q[:, cols]), kw[:, cols]) + bias_ref[g]
        kidx = lax.broadcasted_iota(jnp.int32, s.shape, 1)
        s = jnp.where(kidx >= first_valid, s, MASK_VALUE)
        m = jnp.max(s, axis=-1, keepdims=True)
        p = jnp.exp(s - m)
        l = jnp.sum(p, axis=-1, keepdims=True)
        outs.append(_gather_heads(_dot(p.astype(BF16), vw[:, cols]) / l))
    o_ref[...] = jnp.concatenate(outs, axis=1).astype(BF16)


def _band_prompt(qa, ka, va, bias):
    b, t, _ = qa.shape
    return pl.pallas_call(
        _band_prompt_kernel,
        grid=(b, t // CHUNK),
        in_specs=[pl.BlockSpec((None, CHUNK, A_WIDTH), lambda i, c: (i, c, 0)),
                  pl.BlockSpec((None, t, A_WIDTH), lambda i, c: (i, 0, 0)),
                  pl.BlockSpec((None, t, A_WIDTH), lambda i, c: (i, 0, 0)),
                  _resident(bias.shape)],
        out_specs=pl.BlockSpec((None, CHUNK, A_WIDTH), lambda i, c: (i, c, 0)),
        out_shape=jax.ShapeDtypeStruct((b, t, A_WIDTH), BF16),
        scratch_shapes=[pltpu.VMEM((A_WIN + t, A_WIDTH), BF16),
                        pltpu.VMEM((A_WIN + t, A_WIDTH), BF16)],
        compiler_params=_params("arbitrary", "arbitrary"),
        name="band_prompt",
    )(qa, ka, va, bias)


def _band_sample_kernel(q_ref, kct_ref, vct_ref, kn_ref, vn_ref, bias_ref, o_ref):
    q = q_ref[...]
    outs = []
    for g in range(A_HEADS // A_GROUP):
        cols = slice(g * A_GROUP_WIDTH, (g + 1) * A_GROUP_WIDTH)
        qbd = _block_diag_query(q[:, cols])
        s_old = _dot(qbd, kct_ref[cols, :].astype(BF16)) + bias_ref[g, :, 0:A_WIN]
        s_new = _dot_nt(qbd, kn_ref[:, cols]) + bias_ref[g, :, A_WIN:A_KEYS]
        m = jnp.maximum(jnp.max(s_old, axis=-1, keepdims=True), jnp.max(s_new, axis=-1, keepdims=True))
        p_old = jnp.exp(s_old - m)
        p_new = jnp.exp(s_new - m)
        l = jnp.sum(p_old, axis=-1, keepdims=True) + jnp.sum(p_new, axis=-1, keepdims=True)
        o = (_dot_nt(p_old.astype(BF16), vct_ref[cols, :].astype(BF16))
             + _dot(p_new.astype(BF16), vn_ref[:, cols]))
        outs.append(_gather_heads(o / l))
    o_ref[...] = jnp.concatenate(outs, axis=1).astype(BF16)


def _band_sample(qa, ka, va, cache_kt, cache_vt, bias, layer):
    b, t, _ = qa.shape
    w = cache_kt.shape[3]
    new = pl.BlockSpec((None, t, A_WIDTH), lambda i: (i, 0, 0))
    old = pl.BlockSpec((None, None, A_WIDTH, w), lambda i: (layer, i, 0, 0))
    return pl.pallas_call(
        _band_sample_kernel,
        grid=(b,),
        in_specs=[new, old, old, new, new, _resident(bias.shape)],
        out_specs=new,
        out_shape=jax.ShapeDtypeStruct((b, t, A_WIDTH), BF16),
        compiler_params=_params("arbitrary"),
        name="band_sample",
    )(qa, cache_kt, cache_vt, ka, va, bias)


def _diff_lambda(lq_ref, lam0):
    lq = lq_ref[...]
    a = jnp.sum(lq[0:1, :] * lq[1:2, :], axis=-1, keepdims=True)
    b = jnp.sum(lq[2:3, :] * lq[3:4, :], axis=-1, keepdims=True)
    return jnp.exp(a) - jnp.exp(b) + lam0


def _stack_components(q):
    lane = lax.broadcasted_iota(jnp.int32, q.shape, 1)
    zero = jnp.zeros_like(q)
    return jnp.concatenate([jnp.where(lane < HEAD_DIM, q, zero),
                            jnp.where(lane >= HEAD_DIM, q, zero)], axis=0)


def _own_block_bias_and_mask(slope, qpos, kpos):
    bias = slope * jnp.minimum(kpos, 2 * qpos - kpos).astype(F32)
    valid = (kpos // CHUNK) <= (qpos // CHUNK)
    return bias, valid


def _diff_prompt_kernel(lam0, tq, tk, slopes_ref, lq_ref, subg_ref, q_ref, k_ref, vt_ref, o_ref,
                        acc_ref):
    h = pl.program_id(1)
    qi = pl.program_id(2)
    slope = slopes_ref[h]
    qs = _stack_components(q_ref[...])
    q_start = qi * tq
    tv = vt_ref.shape[2]
    lane_reps = 2 * tq // LANES
    key_iota = lax.broadcasted_iota(jnp.int32, (tk, LANES), 0)

    acc_ref[...] = jnp.zeros(acc_ref.shape, F32)

    def update(s, first_v_block, m_old, l_old):
        m_new = jnp.maximum(m_old, jnp.max(s, axis=0, keepdims=True))
        alpha = jnp.exp(m_old - m_new)
        p = jnp.exp(s - m_new)
        l_new = alpha * l_old + jnp.sum(p, axis=0, keepdims=True)
        pb = p.astype(BF16)
        pv = _dot(vt_ref[first_v_block], pb[0:tv])
        for u in range(1, tk // tv):
            pv = pv + _dot(vt_ref[first_v_block + u], pb[u * tv:(u + 1) * tv])
        acc_ref[...] = alpha * acc_ref[...] + pv
        return m_new, l_new

    def below(j, carry):
        ks = pl.multiple_of(j * tk, tk)
        kbias = slope * (key_iota + (ks - q_start)).astype(F32)
        s = _dot_nt(k_ref[pl.ds(ks, tk), :], qs) + pltpu.repeat(kbias, lane_reps, axis=1)
        return update(s, j * (tk // tv), *carry)

    m0 = jnp.full((1, 2 * tq), MASK_VALUE, F32)
    l0 = jnp.zeros((1, 2 * tq), F32)
    m, l = lax.fori_loop(0, q_start // tk, below, (m0, l0))

    for d in range(tq // tk):
        ks = pl.multiple_of(q_start + d * tk, tk)
        kpos = lax.broadcasted_iota(jnp.int32, (tk, tq), 0) + d * tk
        qpos = lax.broadcasted_iota(jnp.int32, (tk, tq), 1)
        bias, valid = _own_block_bias_and_mask(slope, qpos, kpos)
        bias = jnp.concatenate([bias, bias], axis=1)
        valid = jnp.concatenate([valid, valid], axis=1)
        s = _dot_nt(k_ref[pl.ds(ks, tk), :], qs) + bias
        m, l = update(jnp.where(valid, s, MASK_VALUE), (q_start + d * tk) // tv, m, l)

    lam = _diff_lambda(lq_ref, lam0)
    acc = acc_ref[...]
    ot = acc[:, :tq] / l[:, :tq] - lam * (acc[:, tq:] / l[:, tq:])
    ot = ot * lax.rsqrt(jnp.mean(ot * ot, axis=0, keepdims=True) + NORM_EPS)
    o_ref[...] = (ot.T * subg_ref[...] * (1.0 - lam0)).astype(BF16)


def _diff_prompt(qb, kb, vbt, lq, subg, lam0, layer, tq, tk):
    b, t, _ = qb.shape
    nblk, tv = vbt.shape[1], vbt.shape[3]
    assert tk % tv == 0 and tq % tk == 0
    blk = pl.BlockSpec((None, tq, B_HEAD_WIDTH), lambda i, h, q: (i, q, h))
    return pl.pallas_call(
        functools.partial(_diff_prompt_kernel, lam0, tq, tk),
        grid=(b, B_HEADS, t // tq),
        in_specs=[pl.BlockSpec(memory_space=pltpu.SMEM), _layer(lq.shape[1:], layer),
                  _layer((1, B_HEAD_WIDTH), layer), blk,
                  pl.BlockSpec((None, t, B_HEAD_WIDTH), lambda i, h, q: (i, 0, h)),
                  pl.BlockSpec((None, nblk, B_HEAD_WIDTH, tv), lambda i, h, q: (i, 0, h, 0))],
        out_specs=blk,
        out_shape=jax.ShapeDtypeStruct((b, t, B_WIDTH), BF16),
        scratch_shapes=[pltpu.VMEM((B_HEAD_WIDTH, 2 * tq), F32)],
        compiler_params=_params("arbitrary", "arbitrary", "arbitrary"),
        name="diff_prompt",
    )(_alibi_slopes(B_HEADS), lq, subg.reshape(subg.shape[0], 1, B_HEAD_WIDTH), qb, kb, vbt)


def _diff_sample_kernel(lam0, nkc, slopes_ref, lq_ref, subg_ref, q_ref, kct_ref, vch_ref, kn_ref,
                        vn_ref, o_ref, m_ref, l_ref, acc_ref):
    kc = pl.program_id(1)
    t = q_ref.shape[0]
    tc = kct_ref.shape[1]
    past = tc * nkc

    @pl.when(kc == 0)
    def _():
        m_ref[...] = jnp.full(m_ref.shape, MASK_VALUE, F32)
        l_ref[...] = jnp.zeros(l_ref.shape, F32)
        acc_ref[...] = jnp.zeros(acc_ref.shape, F32)

    def update(h, s, v):
        width = s.shape[1]
        m_old = m_ref[h]
        m_new = jnp.maximum(m_old, jnp.max(s, axis=1, keepdims=True))
        alpha = jnp.exp(m_old - m_new)
        if width % LANES == 0:
            p = jnp.exp(s - pltpu.repeat(m_new, width // LANES, axis=1))
        else:
            p = jnp.exp(s - m_new[:, :width])
        l_ref[h] = alpha * l_ref[h] + jnp.sum(p, axis=1, keepdims=True)
        acc_ref[h] = alpha * acc_ref[h] + _dot(p.astype(BF16), v)
        m_ref[h] = m_new

    kpos = (kc * tc - past + lax.broadcasted_iota(jnp.int32, (1, tc), 1)).astype(F32)
    for h in range(B_HEADS):
        cols = slice(h * B_HEAD_WIDTH, (h + 1) * B_HEAD_WIDTH)
        qs = _stack_components(q_ref[:, cols])
        s = _dot(qs, kct_ref[cols, :].astype(BF16)) + slopes_ref[h] * kpos
        update(h, s, vch_ref[pl.ds(h, tc, stride=B_HEADS), :].astype(BF16))

    @pl.when(kc == nkc - 1)
    def _():
        lam = _diff_lambda(lq_ref, lam0)
        qpos = lax.broadcasted_iota(jnp.int32, (2 * t, t), 0) % t
        kidx = lax.broadcasted_iota(jnp.int32, (2 * t, t), 1)
        for h in range(B_HEADS):
            cols = slice(h * B_HEAD_WIDTH, (h + 1) * B_HEAD_WIDTH)
            qs = _stack_components(q_ref[:, cols])
            bias, _ = _own_block_bias_and_mask(slopes_ref[h], qpos, kidx)
            update(h, _dot_nt(qs, kn_ref[:, cols]) + bias, vn_ref[:, cols])
            acc = acc_ref[h]
            l = l_ref[h]
            o = acc[:t] / l[:t] - lam * (acc[t:] / l[t:])
            o_ref[:, cols] = (_rms(o, subg_ref[...]) * (1.0 - lam0)).astype(BF16)


def _diff_sample(qb, kb, vb, cache_kt, cache_vh, lq, subg, lam0, layer, tc):
    b, t, _ = qb.shape
    past = cache_kt.shape[3]
    new = pl.BlockSpec((None, t, B_WIDTH), lambda i, k: (i, 0, 0))
    stat = pltpu.VMEM((B_HEADS, 2 * t, LANES), F32)
    return pl.pallas_call(
        functools.partial(_diff_sample_kernel, lam0, past // tc),
        grid=(b, past // tc),
        in_specs=[pl.BlockSpec(memory_space=pltpu.SMEM), _layer(lq.shape[1:], layer),
                  _layer((1, B_HEAD_WIDTH), layer), new,
                  pl.BlockSpec((None, None, B_WIDTH, tc), lambda i, k: (layer, i, 0, k)),
                  pl.BlockSpec((None, None, tc * B_HEADS, B_HEAD_WIDTH), lambda i, k: (layer, i, k, 0)),
                  new, new],
        out_specs=new,
        out_shape=jax.ShapeDtypeStruct((b, t, B_WIDTH), BF16),
        scratch_shapes=[stat, stat, pltpu.VMEM((B_HEADS, 2 * t, B_HEAD_WIDTH), F32)],
        compiler_params=_params("arbitrary", "arbitrary"),
        name="diff_sample",
    )(_alibi_slopes(B_HEADS), lq, subg.reshape(subg.shape[0], 1, B_HEAD_WIDTH), qb, cache_kt, cache_vh,
      kb, vb)


def _merge_kernel(ff_chunk, h_ref, gate_ref, oa_ref, ob_ref, ple_ref,
                  wba_ref, wbb_ref, wout_ref, wup_ref, wdown_ref, wple_ref, wpg_ref,
                  g_post_ref, g_fpre_ref, g_fpost_ref, g_ple_ref, o_ref):
    d = h_ref.shape[1]
    gate = gate_ref[...]
    mix = (gate[:, :d].astype(F32) * _dot(oa_ref[...], wba_ref[...])
           + gate[:, d:].astype(F32) * _dot(ob_ref[...], wbb_ref[...]))
    h = h_ref[...] + _rms(_dot(mix.astype(BF16), wout_ref[...]), g_post_ref[...])

    u = _rms(h, g_fpre_ref[...]).astype(BF16)
    f = jnp.zeros(h.shape, F32)
    for lo in range(0, wup_ref.shape[1], ff_chunk):
        act = jnp.square(jnp.maximum(_dot(u, wup_ref[:, lo:lo + ff_chunk]), 0.0))
        f = f + _dot(act.astype(BF16), wdown_ref[lo:lo + ff_chunk, :])
    h = h + _rms(f, g_fpost_ref[...])

    pg = jax.nn.sigmoid(_dot(h.astype(BF16), wpg_ref[...]))
    pe = _rms(_dot(ple_ref[...].astype(BF16), wple_ref[...]), g_ple_ref[...])
    o_ref[...] = h + pg * pe


def _merge(h, gate, oa, ob, ple, w, gains, layer, tm, ff_chunk):
    n, d = h.shape
    row = lambda a: pl.BlockSpec((tm, a.shape[1]), lambda i: (i, 0))
    acts = [h, gate, oa, ob]
    gains = [g.reshape(g.shape[0], 1, d) for g in gains]
    return pl.pallas_call(
        functools.partial(_merge_kernel, ff_chunk),
        grid=(n // tm,),
        in_specs=([row(a) for a in acts]
                  + [pl.BlockSpec((None, tm, ple.shape[2]), lambda i: (layer, i, 0))]
                  + [_layer(x.shape[1:], layer) for x in w] + [_layer((1, d), layer)] * 4),
        out_specs=pl.BlockSpec((tm, d), lambda i: (i, 0)),
        out_shape=jax.ShapeDtypeStruct((n, d), F32),
        compiler_params=_params("arbitrary"),
        name="merge",
    )(*acts, ple, *w, *gains)


ROW_TILE = 256
FF_CHUNK = 1024
DIFF_TQ = 512
DIFF_TK = 512
DIFF_SAMPLE_TC = 1024


def kernel(x_prompt, x_sample, cache_a_k, cache_a_v, cache_b_k, cache_b_v, p_prompt, p_sample,
           w_in, rel_bias, diff_lambda, diff_subln, w_gate, w_branch_a, w_branch_b, w_out,
           g_mix_pre, g_mix_post, g_ffn_pre, g_ffn_post, w_up, w_down, w_ple, g_ple, w_ple_gate):
    depth = w_in.shape[0]
    bp, tp, d = x_prompt.shape
    bs, ts, _ = x_sample.shape
    past = cache_b_k.shape[2]
    a_win = cache_a_k.shape[2]
    keep = min(A_WIN, tp)
    assert a_win == A_WIN and keep == A_WIN and ts == CHUNK and tp % DIFF_TQ == 0

    w_in_b, w_gate_b = w_in.astype(BF16), w_gate.astype(BF16)
    merge_w = tuple(x.astype(BF16) for x in (w_branch_a, w_branch_b, w_out, w_up, w_down, w_ple,
                                             w_ple_gate))
    gains = (g_mix_post, g_ffn_pre, g_ffn_post, g_ple)

    cache_a_kt = jnp.transpose(cache_a_k, (0, 1, 3, 4, 2)).reshape(depth, bs, A_WIDTH, a_win)
    cache_a_vt = jnp.transpose(cache_a_v, (0, 1, 3, 4, 2)).reshape(depth, bs, A_WIDTH, a_win)
    cache_b_kt = jnp.transpose(cache_b_k, (0, 1, 3, 4, 5, 2)).reshape(depth, bs, B_WIDTH, past)
    cache_b_vh = cache_b_v.reshape(depth, bs, past * B_HEADS, B_HEAD_WIDTH)
    ple_p = p_prompt.reshape(depth, bp * tp, -1)
    ple_s = p_sample.reshape(depth, bs * ts, -1)

    hp = x_prompt.reshape(bp * tp, d)
    hs = x_sample.reshape(bs * ts, d)
    outs = [[] for _ in range(8)]

    for i in range(depth):
        lam0 = _lambda_init(i)
        bias = _band_bias(rel_bias[i])

        akt, avt = _proj_tail(hp.reshape(bp, tp, d), g_mix_pre, w_in_b, i, keep)
        qa, kab, vab, qb, kbb, kbt, vbt, vbh, gate = _proj(hp, g_mix_pre, w_in_b, w_gate_b, i,
                                                           ROW_TILE, tp)
        r3 = lambda a: a.reshape(bp, tp, a.shape[-1])
        oa = _band_prompt(r3(qa), r3(kab), r3(vab), bias)
        ob = _diff_prompt(r3(qb), r3(kbb), vbt, diff_lambda, diff_subln, lam0, i, DIFF_TQ, DIFF_TK)
        hp = _merge(hp, gate, oa.reshape(bp * tp, -1), ob.reshape(bp * tp, -1), ple_p, merge_w, gains,
                    i, ROW_TILE, FF_CHUNK)
        outs[0].append(akt)
        outs[1].append(avt)
        outs[2].append(kbt)
        outs[3].append(vbh)

        qa, kab, vab, qb, kbb, kaf, vaf, kbf, vbb, vbh, gate = _proj(hs, g_mix_pre, w_in_b, w_gate_b,
                                                                     i, ROW_TILE, None)
        r3 = lambda a: a.reshape(bs, ts, a.shape[-1])
        oa = _band_sample(r3(qa), r3(kab), r3(vab), cache_a_kt, cache_a_vt, bias, i)
        ob = _diff_sample(r3(qb), r3(kbb), r3(vbb), cache_b_kt, cache_b_vh, diff_lambda, diff_subln,
                          lam0, i, DIFF_SAMPLE_TC)
        hs = _merge(hs, gate, oa.reshape(bs * ts, -1), ob.reshape(bs * ts, -1), ple_s, merge_w, gains,
                    i, ROW_TILE, FF_CHUNK)
        outs[4].append(kaf)
        outs[5].append(vaf)
        outs[6].append(kbf)
        outs[7].append(vbh)

    st = [jnp.stack(o) for o in outs]
    a_feature_major = lambda a: jnp.transpose(a.reshape(depth, bp, A_HEADS, HEAD_DIM, keep), (0, 1, 4, 2, 3))
    return (hp.reshape(bp, tp, d), hs.reshape(bs, ts, d),
            a_feature_major(st[0]), a_feature_major(st[1]),
            jnp.transpose(st[2].reshape(depth, bp, B_HEADS, 2, HEAD_DIM, tp), (0, 1, 5, 2, 3, 4)),
            st[3].reshape(depth, bp, tp, B_HEADS, B_HEAD_WIDTH),
            st[4].reshape(depth, bs, ts, A_HEADS, HEAD_DIM), st[5].reshape(depth, bs, ts, A_HEADS, HEAD_DIM),
            st[6].reshape(depth, bs, ts, B_HEADS, 2, HEAD_DIM),
            st[7].reshape(depth, bs, ts, B_HEADS, B_HEAD_WIDTH))
```

```python
import functools
import math

import jax
import jax.numpy as jnp
import numpy as np
from jax import lax
from jax.experimental import pallas as pl
from jax.experimental.pallas import tpu as pltpu

CHUNK = 64
HEAD_DIM = 64
A_HEADS = 8
A_BAND = 8
A_WIDTH = A_HEADS * HEAD_DIM
A_WIN = A_BAND * CHUNK
A_KEYS = A_WIN + CHUNK
REL_CLIP = 128
B_HEADS = 8
B_HEAD_WIDTH = 2 * HEAD_DIM
B_WIDTH = B_HEADS * B_HEAD_WIDTH
NORM_EPS = 1e-6
MASK_VALUE = -1e30
QK_SCALE = HEAD_DIM ** -0.5
LOG2E = math.log2(math.e)
SUM_ROWS = 16

A_GROUP = 4
A_GROUP_WIDTH = A_GROUP * HEAD_DIM
LANES = 128

V7X_VMEM_LIMIT_BYTES = 60000 * 1024

BF16 = jnp.bfloat16
F32 = jnp.float32


def _lambda_init(layer):
    return 0.8 - 0.6 * math.exp(-0.3 * layer)


def _alibi_slopes(n):
    return jnp.asarray(2.0 ** (-8.0 * np.arange(1, n + 1) / n), dtype=F32)


def _rms(x, g):
    return x * lax.rsqrt(jnp.mean(x * x, axis=-1, keepdims=True) + NORM_EPS) * g


def _dot(a, b):
    return jnp.dot(a, b, preferred_element_type=F32)


def _dot_nt(a, b):
    return lax.dot_general(a, b, (((1,), (1,)), ((), ())), preferred_element_type=F32)


def _resident(shape):
    nd = len(shape)
    return pl.BlockSpec(shape, lambda *_: (0,) * nd, pipeline_mode=pl.Buffered(1))


def _layer(shape, layer):
    nd = len(shape)
    return pl.BlockSpec((None,) + tuple(shape), lambda *_: (layer,) + (0,) * nd,
                        pipeline_mode=pl.Buffered(1))


def _params(*sem):
    return pltpu.CompilerParams(dimension_semantics=sem, vmem_limit_bytes=V7X_VMEM_LIMIT_BYTES)


IN_QA, IN_KA, IN_VA = 0, A_WIDTH, 2 * A_WIDTH
IN_QB, IN_KB, IN_VB = 3 * A_WIDTH, 3 * A_WIDTH + B_WIDTH, 3 * A_WIDTH + 2 * B_WIDTH


def _store_head_tiled(o_ref, x):
    rows = x.shape[0]
    for h in range(B_HEADS):
        o_ref[pl.ds(h, rows, stride=B_HEADS), :] = x[:, h * B_HEAD_WIDTH:(h + 1) * B_HEAD_WIDTH]


def _proj_kernel(prompt, n_carried, h_ref, g_ref, win_ref, wg_ref, *refs):
    out_refs = refs[n_carried:]
    u = _rms(h_ref[...], g_ref[...]).astype(BF16)

    def col(lo, width):
        return _dot(u, win_ref[:, lo:lo + width])

    qa = (col(IN_QA, A_WIDTH) * QK_SCALE).astype(BF16)
    ka = col(IN_KA, A_WIDTH)
    va = col(IN_VA, A_WIDTH)
    qb = (col(IN_QB, B_WIDTH) * (QK_SCALE * LOG2E)).astype(BF16)
    kb = col(IN_KB, B_WIDTH)
    vb = col(IN_VB, B_WIDTH)
    gate = jax.nn.sigmoid(_dot(u, wg_ref[...])).astype(BF16)
    if prompt:
        qa_ref, kab_ref, vab_ref, qb_ref, kbb_ref, kbt_ref, vbt_ref, vbh_ref, gate_ref = out_refs
        kbt_ref[...] = kb.T
        vbt_ref[...] = vb.T.astype(BF16)
    else:
        (qa_ref, kab_ref, vab_ref, qb_ref, kbb_ref, kaf_ref, vaf_ref, kbf_ref, vbb_ref, vbh_ref,
         gate_ref) = out_refs
        kaf_ref[...] = ka
        vaf_ref[...] = va
        kbf_ref[...] = kb
        vbb_ref[...] = vb.astype(BF16)
    qa_ref[...] = qa
    kab_ref[...] = ka.astype(BF16)
    vab_ref[...] = va.astype(BF16)
    qb_ref[...] = qb
    kbb_ref[...] = kb.astype(BF16)
    _store_head_tiled(vbh_ref, vb)
    gate_ref[...] = gate


def _proj(h, g, w_in, w_gate, layer, tm, prompt_seq, carried):
    n, d = h.shape
    depth, _, gate_w = w_gate.shape
    row = lambda w: pl.BlockSpec((tm, w), lambda i: (i, 0))
    sds = jax.ShapeDtypeStruct
    tok = lambda w, dt: (row(w), sds((n, w), dt))
    tok_stacked = lambda w: (pl.BlockSpec((None, tm, w), lambda i: (layer, i, 0)), sds((depth, n, w), F32))
    head_tiled = (pl.BlockSpec((None, tm * B_HEADS, B_HEAD_WIDTH), lambda i: (layer, i, 0)),
                  sds((depth, n * B_HEADS, B_HEAD_WIDTH), F32))
    local = [tok(A_WIDTH, BF16), tok(A_WIDTH, BF16), tok(A_WIDTH, BF16), tok(B_WIDTH, BF16),
             tok(B_WIDTH, BF16)]
    if prompt_seq is not None:
        nblk = prompt_seq // tm
        b = n // prompt_seq
        outs = local + [
            (pl.BlockSpec((None, None, B_WIDTH, tm), lambda i: (layer, i // nblk, 0, i % nblk)),
             sds((depth, b, B_WIDTH, prompt_seq), F32)),
            (pl.BlockSpec((None, None, B_WIDTH, tm), lambda i: (i // nblk, i % nblk, 0, 0)),
             sds((b, nblk, B_WIDTH, tm), BF16)),
            head_tiled, tok(gate_w, BF16)]
        stacked_idx = [5, 7]
    else:
        outs = local + [tok_stacked(A_WIDTH), tok_stacked(A_WIDTH), tok_stacked(B_WIDTH),
                        tok(B_WIDTH, BF16), head_tiled, tok(gate_w, BF16)]
        stacked_idx = [5, 6, 7, 9]
    n_fixed_inputs = 4
    carried = [] if carried is None else list(carried)
    aliases = {n_fixed_inputs + k: stacked_idx[k] for k in range(len(carried))}
    res = pl.pallas_call(
        functools.partial(_proj_kernel, prompt_seq is not None, len(carried)),
        grid=(n // tm,),
        in_specs=[row(d), _layer((1, d), layer), _layer(w_in.shape[1:], layer),
                  _layer(w_gate.shape[1:], layer)] + [pl.BlockSpec(memory_space=pl.ANY)] * len(carried),
        out_specs=[s for s, _ in outs],
        out_shape=[s for _, s in outs],
        input_output_aliases=aliases,
        compiler_params=_params("arbitrary"),
        name="proj",
    )(h, g.reshape(g.shape[0], 1, d), w_in, w_gate, *carried)
    return ([r for k, r in enumerate(res) if k not in stacked_idx], [res[k] for k in stacked_idx])


def _proj_tail_kernel(h_ref, g_ref, wk_ref, wv_ref, kt_ref, vt_ref):
    u = _rms(h_ref[...], g_ref[...]).astype(BF16)
    kt_ref[...] = _dot(u, wk_ref[...]).T
    vt_ref[...] = _dot(u, wv_ref[...]).T


def _proj_tail(h3, g, w_in, layer, keep):
    b, t, d = h3.shape
    last = t // keep - 1
    wcol = lambda c: pl.BlockSpec((None, d, A_WIDTH), lambda i: (layer, 0, c), pipeline_mode=pl.Buffered(1))
    out = pl.BlockSpec((None, A_WIDTH, keep), lambda i: (i, 0, 0))
    return pl.pallas_call(
        _proj_tail_kernel,
        grid=(b,),
        in_specs=[pl.BlockSpec((None, keep, d), lambda i: (i, last, 0)), _layer((1, d), layer),
                  wcol(IN_KA // A_WIDTH), wcol(IN_VA // A_WIDTH)],
        out_specs=[out, out],
        out_shape=[jax.ShapeDtypeStruct((b, A_WIDTH, keep), F32)] * 2,
        compiler_params=_params("arbitrary"),
        name="proj_tail",
    )(h3, g.reshape(g.shape[0], 1, d), w_in, w_in)


def _band_bias_kernel(tab_ref, o_ref):
    qi = lax.broadcasted_iota(jnp.int32, (CHUNK, A_KEYS), 0)
    ki = lax.broadcasted_iota(jnp.int32, (CHUNK, A_KEYS), 1)
    idx = jnp.clip(A_WIN + qi - ki, -REL_CLIP, REL_CLIP) + REL_CLIP
    lo = REL_CLIP - CHUNK + 1
    hi = 2 * REL_CLIP

    for h in range(A_HEADS):
        g, hh = divmod(h, A_GROUP)
        o_ref[g, hh * CHUNK:(hh + 1) * CHUNK, :] = jnp.full((CHUNK, A_KEYS), tab_ref[h, hi], F32)

    def body(r, carry):
        hit = idx == r
        for h in range(A_HEADS):
            g, hh = divmod(h, A_GROUP)
            sl = (g, slice(hh * CHUNK, (hh + 1) * CHUNK), slice(None))
            o_ref[sl] = jnp.where(hit, tab_ref[h, r], o_ref[sl])
        return carry

    lax.fori_loop(lo, hi, body, 0)


def _band_bias(table):
    return pl.pallas_call(
        _band_bias_kernel,
        in_specs=[pl.BlockSpec(memory_space=pltpu.SMEM)],
        out_specs=pl.BlockSpec(memory_space=pltpu.VMEM),
        out_shape=jax.ShapeDtypeStruct((A_HEADS // A_GROUP, A_GROUP * CHUNK, A_KEYS), F32),
        name="band_bias",
    )(table)


def _block_diag_query(qg):
    lane_head = lax.broadcasted_iota(jnp.int32, qg.shape, 1) // HEAD_DIM
    return jnp.concatenate(
        [jnp.where(lane_head == hh, qg, jnp.zeros_like(qg)) for hh in range(A_GROUP)], axis=0)


def _gather_heads(o):
    lane_head = lax.broadcasted_iota(jnp.int32, (CHUNK, A_GROUP_WIDTH), 1) // HEAD_DIM
    og = jnp.zeros((CHUNK, A_GROUP_WIDTH), F32)
    for hh in range(A_GROUP):
        og = og + jnp.where(lane_head == hh, o[hh * CHUNK:(hh + 1) * CHUNK, :], 0.0)
    return og


def _band_prompt_kernel(q_ref, k_ref, v_ref, bias_ref, o_ref, kpad_ref, vpad_ref):
    c = pl.program_id(1)
    t = k_ref.shape[0]

    @pl.when(c == 0)
    def _():
        kpad_ref[0:A_WIN, :] = jnp.zeros((A_WIN, A_WIDTH), BF16)
        vpad_ref[0:A_WIN, :] = jnp.zeros((A_WIN, A_WIDTH), BF16)
        kpad_ref[A_WIN:A_WIN + t, :] = k_ref[...]
        vpad_ref[A_WIN:A_WIN + t, :] = v_ref[...]

    start = pl.multiple_of(c * CHUNK, CHUNK)
    kw = kpad_ref[pl.ds(start, A_KEYS), :]
    vw = vpad_ref[pl.ds(start, A_KEYS), :]
    first_valid = (A_BAND - c) * CHUNK
    q = q_ref[...]
    outs = []
    for g in range(A_HEADS // A_GROUP):
        cols = slice(g * A_GROUP_WIDTH, (g + 1) * A_GROUP_WIDTH)
        s = _dot_nt(_block_diag_query(q[:, cols]), kw[:, cols]) + bias_ref[g]
        kidx = lax.broadcasted_iota(jnp.int32, s.shape, 1)
        s = jnp.where(kidx >= first_valid, s, MASK_VALUE)
        m = jnp.max(s, axis=-1, keepdims=True)
        p = jnp.exp(s - m)
        l = jnp.sum(p, axis=-1, keepdims=True)
        outs.append(_gather_heads(_dot(p.astype(BF16), vw[:, cols]) / l))
    o_ref[...] = jnp.concatenate(outs, axis=1).astype(BF16)


def _band_prompt(qa, ka, va, bias):
    b, t, _ = qa.shape
    return pl.pallas_call(
        _band_prompt_kernel,
        grid=(b, t // CHUNK),
        in_specs=[pl.BlockSpec((None, CHUNK, A_WIDTH), lambda i, c: (i, c, 0)),
                  pl.BlockSpec((None, t, A_WIDTH), lambda i, c: (i, 0, 0)),
                  pl.BlockSpec((None, t, A_WIDTH), lambda i, c: (i, 0, 0)),
                  _resident(bias.shape)],
        out_specs=pl.BlockSpec((None, CHUNK, A_WIDTH), lambda i, c: (i, c, 0)),
        out_shape=jax.ShapeDtypeStruct((b, t, A_WIDTH), BF16),
        scratch_shapes=[pltpu.VMEM((A_WIN + t, A_WIDTH), BF16),
                        pltpu.VMEM((A_WIN + t, A_WIDTH), BF16)],
        compiler_params=_params("arbitrary", "arbitrary"),
        name="band_prompt",
    )(qa, ka, va, bias)


def _band_sample_kernel(q_ref, kct_ref, vct_ref, kn_ref, vn_ref, bias_ref, o_ref):
    q = q_ref[...]
    outs = []
    for g in range(A_HEADS // A_GROUP):
        cols = slice(g * A_GROUP_WIDTH, (g + 1) * A_GROUP_WIDTH)
        qbd = _block_diag_query(q[:, cols])
        s_old = _dot(qbd, kct_ref[cols, :].astype(BF16)) + bias_ref[g, :, 0:A_WIN]
        s_new = _dot_nt(qbd, kn_ref[:, cols]) + bias_ref[g, :, A_WIN:A_KEYS]
        m = jnp.maximum(jnp.max(s_old, axis=-1, keepdims=True), jnp.max(s_new, axis=-1, keepdims=True))
        p_old = jnp.exp(s_old - m)
        p_new = jnp.exp(s_new - m)
        l = jnp.sum(p_old, axis=-1, keepdims=True) + jnp.sum(p_new, axis=-1, keepdims=True)
        o = (_dot_nt(p_old.astype(BF16), vct_ref[cols, :].astype(BF16))
             + _dot(p_new.astype(BF16), vn_ref[:, cols]))
        outs.append(_gather_heads(o / l))
    o_ref[...] = jnp.concatenate(outs, axis=1).astype(BF16)


def _band_sample(qa, ka, va, cache_kt, cache_vt, bias, layer):
    b, t, _ = qa.shape
    w = cache_kt.shape[3]
    new = pl.BlockSpec((None, t, A_WIDTH), lambda i: (i, 0, 0))
    old = pl.BlockSpec((None, None, A_WIDTH, w), lambda i: (layer, i, 0, 0))
    return pl.pallas_call(
        _band_sample_kernel,
        grid=(b,),
        in_specs=[new, old, old, new, new, _resident(bias.shape)],
        out_specs=new,
        out_shape=jax.ShapeDtypeStruct((b, t, A_WIDTH), BF16),
        compiler_params=_params("arbitrary"),
        name="band_sample",
    )(qa, cache_kt, cache_vt, ka, va, bias)


def _diff_lambda(lq_ref, lam0):
    lq = lq_ref[...]
    a = jnp.sum(lq[0:1, :] * lq[1:2, :], axis=-1, keepdims=True)
    b = jnp.sum(lq[2:3, :] * lq[3:4, :], axis=-1, keepdims=True)
    return jnp.exp(a) - jnp.exp(b) + lam0


def _stack_components(q):
    lane = lax.broadcasted_iota(jnp.int32, q.shape, 1)
    zero = jnp.zeros_like(q)
    return jnp.concatenate([jnp.where(lane < HEAD_DIM, q, zero),
                            jnp.where(lane >= HEAD_DIM, q, zero)], axis=0)


def _own_block_bias_and_mask(slope, qpos, kpos):
    bias = slope * jnp.minimum(kpos, 2 * qpos - kpos).astype(F32)
    valid = (kpos // CHUNK) <= (qpos // CHUNK)
    return bias, valid


def _diff_prompt_kernel(lam0, hb, tq, tk, slopes_ref, lq_ref, subg_ref, q_ref, k_ref, vt_ref, o_ref,
                        acc_ref):
    hg = pl.program_id(1)
    qi = pl.program_id(2)
    q_start = qi * tq
    tv = vt_ref.shape[2]
    lane_reps = 2 * tq // LANES
    key_iota = lax.broadcasted_iota(jnp.int32, (tk, LANES), 0)
    ones_rows = jnp.ones((SUM_ROWS, tv), BF16)
    head_cols = [slice(hh * B_HEAD_WIDTH, (hh + 1) * B_HEAD_WIDTH) for hh in range(hb)]
    slopes = [slopes_ref[hg * hb + hh] * LOG2E for hh in range(hb)]
    qs = [_stack_components(q_ref[:, c]) for c in head_cols]

    acc_ref[...] = jnp.zeros(acc_ref.shape, F32)

    def update(hh, s, first_v_block, m_old):
        m_new = jnp.maximum(m_old, jnp.max(s, axis=0, keepdims=True))
        alpha = jnp.exp2(m_old - m_new)
        pb = jnp.exp2(s - m_new).astype(BF16)
        pv = None
        for u in range(tk // tv):
            vt = jnp.concatenate([vt_ref[first_v_block + u, head_cols[hh], :], ones_rows], axis=0)
            part = _dot(vt, pb[u * tv:(u + 1) * tv])
            pv = part if pv is None else pv + part
        acc_ref[hh] = alpha * acc_ref[hh] + pv
        return m_new

    def below(j, ms):
        ks = pl.multiple_of(j * tk, tk)
        out = []
        for hh in range(hb):
            kbias = slopes[hh] * (key_iota + (ks - q_start)).astype(F32)
            s = _dot_nt(k_ref[pl.ds(ks, tk), head_cols[hh]], qs[hh]) + jnp.tile(kbias, (1, lane_reps))
            out.append(update(hh, s, j * (tk // tv), ms[hh]))
        return tuple(out)

    m0 = jnp.full((1, 2 * tq), MASK_VALUE, F32)
    ms = lax.fori_loop(0, q_start // tk, below, (m0,) * hb)

    for d in range(tq // tk):
        ks = pl.multiple_of(q_start + d * tk, tk)
        kpos = lax.broadcasted_iota(jnp.int32, (tk, tq), 0) + d * tk
        qpos = lax.broadcasted_iota(jnp.int32, (tk, tq), 1)
        out = []
        for hh in range(hb):
            bias, valid = _own_block_bias_and_mask(slopes[hh], qpos, kpos)
            bias = jnp.concatenate([bias, bias], axis=1)
            valid = jnp.concatenate([valid, valid], axis=1)
            s = _dot_nt(k_ref[pl.ds(ks, tk), head_cols[hh]], qs[hh]) + bias
            out.append(update(hh, jnp.where(valid, s, MASK_VALUE), (q_start + d * tk) // tv, ms[hh]))
        ms = tuple(out)

    lam = _diff_lambda(lq_ref, lam0)
    for hh in range(hb):
        acc = acc_ref[hh]
        num = acc[0:B_HEAD_WIDTH]
        l = acc[B_HEAD_WIDTH:B_HEAD_WIDTH + 1]
        ot = num[:, :tq] / l[:, :tq] - lam * (num[:, tq:] / l[:, tq:])
        ot = ot * lax.rsqrt(jnp.mean(ot * ot, axis=0, keepdims=True) + NORM_EPS)
        o_ref[:, head_cols[hh]] = (ot.T * subg_ref[...] * (1.0 - lam0)).astype(BF16)


def _diff_prompt(qb, kb, vbt, lq, subg, lam0, layer, hb, tq, tk):
    b, t, _ = qb.shape
    nblk, tv = vbt.shape[1], vbt.shape[3]
    assert tk % tv == 0 and tq % tk == 0 and B_HEADS % hb == 0
    w = hb * B_HEAD_WIDTH
    blk = pl.BlockSpec((None, tq, w), lambda i, h, q: (i, q, h))
    return pl.pallas_call(
        functools.partial(_diff_prompt_kernel, lam0, hb, tq, tk),
        grid=(b, B_HEADS // hb, t // tq),
        in_specs=[pl.BlockSpec(memory_space=pltpu.SMEM), _layer(lq.shape[1:], layer),
                  _layer((1, B_HEAD_WIDTH), layer), blk,
                  pl.BlockSpec((None, t, w), lambda i, h, q: (i, 0, h)),
                  pl.BlockSpec((None, nblk, w, tv), lambda i, h, q: (i, 0, h, 0))],
        out_specs=blk,
        out_shape=jax.ShapeDtypeStruct((b, t, B_WIDTH), BF16),
        scratch_shapes=[pltpu.VMEM((hb, B_HEAD_WIDTH + SUM_ROWS, 2 * tq), F32)],
        compiler_params=_params("arbitrary", "arbitrary", "arbitrary"),
        name="diff_prompt",
    )(_alibi_slopes(B_HEADS), lq, subg.reshape(subg.shape[0], 1, B_HEAD_WIDTH), qb, kb, vbt)


def _diff_sample_kernel(lam0, nkc, slopes_ref, lq_ref, subg_ref, q_ref, kct_ref, vch_ref, kn_ref,
                        vn_ref, o_ref, m_ref, l_ref, acc_ref):
    kc = pl.program_id(1)
    t = q_ref.shape[0]
    tc = kct_ref.shape[1]
    past = tc * nkc

    @pl.when(kc == 0)
    def _():
        m_ref[...] = jnp.full(m_ref.shape, MASK_VALUE, F32)
        l_ref[...] = jnp.zeros(l_ref.shape, F32)
        acc_ref[...] = jnp.zeros(acc_ref.shape, F32)

    def update(h, s, v):
        width = s.shape[1]
        m_old = m_ref[h]
        m_new = jnp.maximum(m_old, jnp.max(s, axis=1, keepdims=True))
        alpha = jnp.exp2(m_old - m_new)
        if width % LANES == 0:
            p = jnp.exp2(s - jnp.tile(m_new, (1, width // LANES)))
        else:
            p = jnp.exp2(s - m_new[:, :width])
        v_and_ones = jnp.concatenate([v, jnp.ones(v.shape, BF16)], axis=1)
        pv = _dot(p.astype(BF16), v_and_ones)
        l_ref[h] = alpha * l_ref[h] + pv[:, B_HEAD_WIDTH:]
        acc_ref[h] = alpha * acc_ref[h] + pv[:, :B_HEAD_WIDTH]
        m_ref[h] = m_new

    kpos = (kc * tc - past + lax.broadcasted_iota(jnp.int32, (1, tc), 1)).astype(F32)
    for h in range(B_HEADS):
        cols = slice(h * B_HEAD_WIDTH, (h + 1) * B_HEAD_WIDTH)
        qs = _stack_components(q_ref[:, cols])
        s = _dot(qs, kct_ref[cols, :].astype(BF16)) + (slopes_ref[h] * LOG2E) * kpos
        update(h, s, vch_ref[pl.ds(h, tc, stride=B_HEADS), :].astype(BF16))

    @pl.when(kc == nkc - 1)
    def _():
        lam = _diff_lambda(lq_ref, lam0)
        qpos = lax.broadcasted_iota(jnp.int32, (2 * t, t), 0) % t
        kidx = lax.broadcasted_iota(jnp.int32, (2 * t, t), 1)
        for h in range(B_HEADS):
            cols = slice(h * B_HEAD_WIDTH, (h + 1) * B_HEAD_WIDTH)
            qs = _stack_components(q_ref[:, cols])
            bias, _ = _own_block_bias_and_mask(slopes_ref[h] * LOG2E, qpos, kidx)
            update(h, _dot_nt(qs, kn_ref[:, cols]) + bias, vn_ref[:, cols])
            acc = acc_ref[h]
            l = l_ref[h]
            o = acc[:t] / l[:t] - lam * (acc[t:] / l[t:])
            o_ref[:, cols] = (_rms(o, subg_ref[...]) * (1.0 - lam0)).astype(BF16)


def _diff_sample(qb, kb, vb, cache_kt, cache_vh, lq, subg, lam0, layer, tc):
    b, t, _ = qb.shape
    past = cache_kt.shape[3]
    new = pl.BlockSpec((None, t, B_WIDTH), lambda i, k: (i, 0, 0))
    stat = pltpu.VMEM((B_HEADS, 2 * t, LANES), F32)
    return pl.pallas_call(
        functools.partial(_diff_sample_kernel, lam0, past // tc),
        grid=(b, past // tc),
        in_specs=[pl.BlockSpec(memory_space=pltpu.SMEM), _layer(lq.shape[1:], layer),
                  _layer((1, B_HEAD_WIDTH), layer), new,
                  pl.BlockSpec((None, None, B_WIDTH, tc), lambda i, k: (layer, i, 0, k)),
                  pl.BlockSpec((None, None, tc * B_HEADS, B_HEAD_WIDTH), lambda i, k: (layer, i, k, 0)),
                  new, new],
        out_specs=new,
        out_shape=jax.ShapeDtypeStruct((b, t, B_WIDTH), BF16),
        scratch_shapes=[stat, stat, pltpu.VMEM((B_HEADS, 2 * t, B_HEAD_WIDTH), F32)],
        compiler_params=_params("arbitrary", "arbitrary"),
        name="diff_sample",
    )(_alibi_slopes(B_HEADS), lq, subg.reshape(subg.shape[0], 1, B_HEAD_WIDTH), qb, cache_kt, cache_vh,
      kb, vb)


def _merge_kernel(ff_chunk, h_ref, gate_ref, oa_ref, ob_ref, ple_ref,
                  wba_ref, wbb_ref, wout_ref, wup_ref, wdown_ref, wple_ref, wpg_ref,
                  g_post_ref, g_fpre_ref, g_fpost_ref, g_ple_ref, o_ref):
    d = h_ref.shape[1]
    gate = gate_ref[...]
    mix = (gate[:, :d].astype(F32) * _dot(oa_ref[...], wba_ref[...])
           + gate[:, d:].astype(F32) * _dot(ob_ref[...], wbb_ref[...]))
    h = h_ref[...] + _rms(_dot(mix.astype(BF16), wout_ref[...]), g_post_ref[...])

    u = _rms(h, g_fpre_ref[...]).astype(BF16)
    f = jnp.zeros(h.shape, F32)
    for lo in range(0, wup_ref.shape[1], ff_chunk):
        act = jnp.square(jnp.maximum(_dot(u, wup_ref[:, lo:lo + ff_chunk]), 0.0))
        f = f + _dot(act.astype(BF16), wdown_ref[lo:lo + ff_chunk, :])
    h = h + _rms(f, g_fpost_ref[...])

    pg = jax.nn.sigmoid(_dot(h.astype(BF16), wpg_ref[...]))
    pe = _rms(_dot(ple_ref[...].astype(BF16), wple_ref[...]), g_ple_ref[...])
    o_ref[...] = h + pg * pe


def _merge(h, gate, oa, ob, ple, w, gains, layer, tm, ff_chunk):
    n, d = h.shape
    row = lambda a: pl.BlockSpec((tm, a.shape[1]), lambda i: (i, 0))
    acts = [h, gate, oa, ob]
    gains = [g.reshape(g.shape[0], 1, d) for g in gains]
    return pl.pallas_call(
        functools.partial(_merge_kernel, ff_chunk),
        grid=(n // tm,),
        in_specs=([row(a) for a in acts]
                  + [pl.BlockSpec((None, tm, ple.shape[2]), lambda i: (layer, i, 0))]
                  + [_layer(x.shape[1:], layer) for x in w] + [_layer((1, d), layer)] * 4),
        out_specs=pl.BlockSpec((tm, d), lambda i: (i, 0)),
        out_shape=jax.ShapeDtypeStruct((n, d), F32),
        compiler_params=_params("arbitrary"),
        name="merge",
    )(*acts, ple, *w, *gains)


ROW_TILE = 256
FF_CHUNK = 1024
DIFF_HEADS = 2
DIFF_TQ = 512
DIFF_TK = 512
DIFF_SAMPLE_TC = 1024


def kernel(x_prompt, x_sample, cache_a_k, cache_a_v, cache_b_k, cache_b_v, p_prompt, p_sample,
           w_in, rel_bias, diff_lambda, diff_subln, w_gate, w_branch_a, w_branch_b, w_out,
           g_mix_pre, g_mix_post, g_ffn_pre, g_ffn_post, w_up, w_down, w_ple, g_ple, w_ple_gate):
    depth = w_in.shape[0]
    bp, tp, d = x_prompt.shape
    bs, ts, _ = x_sample.shape
    past = cache_b_k.shape[2]
    a_win = cache_a_k.shape[2]
    keep = min(A_WIN, tp)
    assert a_win == A_WIN and keep == A_WIN and ts == CHUNK and tp % DIFF_TQ == 0

    w_in_b, w_gate_b = w_in.astype(BF16), w_gate.astype(BF16)
    merge_w = tuple(x.astype(BF16) for x in (w_branch_a, w_branch_b, w_out, w_up, w_down, w_ple,
                                             w_ple_gate))
    gains = (g_mix_post, g_ffn_pre, g_ffn_post, g_ple)

    cache_a_kt = jnp.transpose(cache_a_k, (0, 1, 3, 4, 2)).reshape(depth, bs, A_WIDTH, a_win)
    cache_a_vt = jnp.transpose(cache_a_v, (0, 1, 3, 4, 2)).reshape(depth, bs, A_WIDTH, a_win)
    cache_b_kt = jnp.transpose(cache_b_k, (0, 1, 3, 4, 5, 2)).reshape(depth, bs, B_WIDTH, past)
    cache_b_vh = cache_b_v.reshape(depth, bs, past * B_HEADS, B_HEAD_WIDTH)
    ple_p = p_prompt.reshape(depth, bp * tp, -1)
    ple_s = p_sample.reshape(depth, bs * ts, -1)

    hp = x_prompt.reshape(bp * tp, d)
    hs = x_sample.reshape(bs * ts, d)
    tails = ([], [])
    stacked_p = stacked_s = None

    for i in range(depth):
        lam0 = _lambda_init(i)
        bias = _band_bias(rel_bias[i])

        akt, avt = _proj_tail(hp.reshape(bp, tp, d), g_mix_pre, w_in_b, i, keep)
        tails[0].append(akt)
        tails[1].append(avt)
        (qa, kab, vab, qb, kbb, vbt, gate), stacked_p = _proj(hp, g_mix_pre, w_in_b, w_gate_b, i,
                                                              ROW_TILE, tp, stacked_p)
        r3 = lambda a: a.reshape(bp, tp, a.shape[-1])
        oa = _band_prompt(r3(qa), r3(kab), r3(vab), bias)
        ob = _diff_prompt(r3(qb), r3(kbb), vbt, diff_lambda, diff_subln, lam0, i, DIFF_HEADS, DIFF_TQ,
                          DIFF_TK)
        hp = _merge(hp, gate, oa.reshape(bp * tp, -1), ob.reshape(bp * tp, -1), ple_p, merge_w, gains,
                    i, ROW_TILE, FF_CHUNK)

        (qa, kab, vab, qb, kbb, vbb, gate), stacked_s = _proj(hs, g_mix_pre, w_in_b, w_gate_b, i,
                                                              ROW_TILE, None, stacked_s)
        r3 = lambda a: a.reshape(bs, ts, a.shape[-1])
        oa = _band_sample(r3(qa), r3(kab), r3(vab), cache_a_kt, cache_a_vt, bias, i)
        ob = _diff_sample(r3(qb), r3(kbb), r3(vbb), cache_b_kt, cache_b_vh, diff_lambda, diff_subln,
                          lam0, i, DIFF_SAMPLE_TC)
        hs = _merge(hs, gate, oa.reshape(bs * ts, -1), ob.reshape(bs * ts, -1), ple_s, merge_w, gains,
                    i, ROW_TILE, FF_CHUNK)

    kbt_p, vbh_p = stacked_p
    kaf_s, vaf_s, kbf_s, vbh_s = stacked_s
    a_feature_major = lambda a: jnp.transpose(
        jnp.stack(a).reshape(depth, bp, A_HEADS, HEAD_DIM, keep), (0, 1, 4, 2, 3))
    return (hp.reshape(bp, tp, d), hs.reshape(bs, ts, d),
            a_feature_major(tails[0]), a_feature_major(tails[1]),
            jnp.transpose(kbt_p.reshape(depth, bp, B_HEADS, 2, HEAD_DIM, tp), (0, 1, 5, 2, 3, 4)),
            vbh_p.reshape(depth, bp, tp, B_HEADS, B_HEAD_WIDTH),
            kaf_s.reshape(depth, bs, ts, A_HEADS, HEAD_DIM), vaf_s.reshape(depth, bs, ts, A_HEADS, HEAD_DIM),
            kbf_s.reshape(depth, bs, ts, B_HEADS, 2, HEAD_DIM),
            vbh_s.reshape(depth, bs, ts, B_HEADS, B_HEAD_WIDTH))
```

```python
import functools
import math

import jax
import jax.numpy as jnp
import numpy as np
from jax import lax
from jax.experimental import pallas as pl
from jax.experimental.pallas import tpu as pltpu

CHUNK = 64
HEAD_DIM = 64
A_HEADS = 8
A_BAND = 8
A_WIDTH = A_HEADS * HEAD_DIM
A_WIN = A_BAND * CHUNK
A_KEYS = A_WIN + CHUNK
REL_CLIP = 128
B_HEADS = 8
B_HEAD_WIDTH = 2 * HEAD_DIM
B_WIDTH = B_HEADS * B_HEAD_WIDTH
NORM_EPS = 1e-6
MASK_VALUE = -1e30
QK_SCALE = HEAD_DIM ** -0.5
LOG2E = math.log2(math.e)
SUM_ROWS = 16

A_GROUP = 4
A_GROUP_WIDTH = A_GROUP * HEAD_DIM
LANES = 128

V7X_VMEM_LIMIT_BYTES = 60000 * 1024

BF16 = jnp.bfloat16
F32 = jnp.float32


def _lambda_init(layer):
    return 0.8 - 0.6 * math.exp(-0.3 * layer)


def _alibi_slopes(n):
    return jnp.asarray(2.0 ** (-8.0 * np.arange(1, n + 1) / n), dtype=F32)


def _rms(x, g):
    return x * lax.rsqrt(jnp.mean(x * x, axis=-1, keepdims=True) + NORM_EPS) * g


def _dot(a, b):
    return jnp.dot(a, b, preferred_element_type=F32)


def _dot_nt(a, b):
    return lax.dot_general(a, b, (((1,), (1,)), ((), ())), preferred_element_type=F32)


def _resident(shape):
    nd = len(shape)
    return pl.BlockSpec(shape, lambda *_: (0,) * nd, pipeline_mode=pl.Buffered(1))


def _layer(shape, layer):
    nd = len(shape)
    return pl.BlockSpec((None,) + tuple(shape), lambda *_: (layer,) + (0,) * nd,
                        pipeline_mode=pl.Buffered(1))


def _params(*sem):
    return pltpu.CompilerParams(dimension_semantics=sem, vmem_limit_bytes=V7X_VMEM_LIMIT_BYTES)


IN_QA, IN_KA, IN_VA = 0, A_WIDTH, 2 * A_WIDTH
IN_QB, IN_KB, IN_VB = 3 * A_WIDTH, 3 * A_WIDTH + B_WIDTH, 3 * A_WIDTH + 2 * B_WIDTH


def _store_head_tiled(o_ref, x):
    rows = x.shape[0]
    for h in range(B_HEADS):
        o_ref[pl.ds(h, rows, stride=B_HEADS), :] = x[:, h * B_HEAD_WIDTH:(h + 1) * B_HEAD_WIDTH]


def _proj_kernel(prompt, n_carried, h_ref, g_ref, win_ref, wg_ref, *refs):
    out_refs = refs[n_carried:]
    u = _rms(h_ref[...], g_ref[...]).astype(BF16)

    def col(lo, width):
        return _dot(u, win_ref[:, lo:lo + width])

    qa = (col(IN_QA, A_WIDTH) * QK_SCALE).astype(BF16)
    ka = col(IN_KA, A_WIDTH)
    va = col(IN_VA, A_WIDTH)
    qb = (col(IN_QB, B_WIDTH) * (QK_SCALE * LOG2E)).astype(BF16)
    kb = col(IN_KB, B_WIDTH)
    vb = col(IN_VB, B_WIDTH)
    gate = jax.nn.sigmoid(_dot(u, wg_ref[...])).astype(BF16)
    if prompt:
        qa_ref, kab_ref, vab_ref, qb_ref, kbb_ref, kbt_ref, vbt_ref, vbh_ref, gate_ref = out_refs
        kbt_ref[...] = kb.T
        vbt_ref[...] = vb.T.astype(BF16)
    else:
        (qa_ref, kab_ref, vab_ref, qb_ref, kbb_ref, kaf_ref, vaf_ref, kbf_ref, vbb_ref, vbh_ref,
         gate_ref) = out_refs
        kaf_ref[...] = ka
        vaf_ref[...] = va
        kbf_ref[...] = kb
        vbb_ref[...] = vb.astype(BF16)
    qa_ref[...] = qa
    kab_ref[...] = ka.astype(BF16)
    vab_ref[...] = va.astype(BF16)
    qb_ref[...] = qb
    kbb_ref[...] = kb.astype(BF16)
    _store_head_tiled(vbh_ref, vb)
    gate_ref[...] = gate


def _proj(h, g, w_in, w_gate, layer, tm, prompt_seq, carried):
    n, d = h.shape
    depth, _, gate_w = w_gate.shape
    row = lambda w: pl.BlockSpec((tm, w), lambda i: (i, 0))
    sds = jax.ShapeDtypeStruct
    tok = lambda w, dt: (row(w), sds((n, w), dt))
    tok_stacked = lambda w: (pl.BlockSpec((None, tm, w), lambda i: (layer, i, 0)), sds((depth, n, w), F32))
    head_tiled = (pl.BlockSpec((None, tm * B_HEADS, B_HEAD_WIDTH), lambda i: (layer, i, 0)),
                  sds((depth, n * B_HEADS, B_HEAD_WIDTH), F32))
    local = [tok(A_WIDTH, BF16), tok(A_WIDTH, BF16), tok(A_WIDTH, BF16), tok(B_WIDTH, BF16),
             tok(B_WIDTH, BF16)]
    if prompt_seq is not None:
        nblk = prompt_seq // tm
        b = n // prompt_seq
        outs = local + [
            (pl.BlockSpec((None, None, B_WIDTH, tm), lambda i: (layer, i // nblk, 0, i % nblk)),
             sds((depth, b, B_WIDTH, prompt_seq), F32)),
            (pl.BlockSpec((None, None, B_WIDTH, tm), lambda i: (i // nblk, i % nblk, 0, 0)),
             sds((b, nblk, B_WIDTH, tm), BF16)),
            head_tiled, tok(gate_w, BF16)]
        stacked_idx = [5, 7]
    else:
        outs = local + [tok_stacked(A_WIDTH), tok_stacked(A_WIDTH), tok_stacked(B_WIDTH),
                        tok(B_WIDTH, BF16), head_tiled, tok(gate_w, BF16)]
        stacked_idx = [5, 6, 7, 9]
    n_fixed_inputs = 4
    carried = [] if carried is None else list(carried)
    aliases = {n_fixed_inputs + k: stacked_idx[k] for k in range(len(carried))}
    res = pl.pallas_call(
        functools.partial(_proj_kernel, prompt_seq is not None, len(carried)),
        grid=(n // tm,),
        in_specs=[row(d), _layer((1, d), layer), _layer(w_in.shape[1:], layer),
                  _layer(w_gate.shape[1:], layer)] + [pl.BlockSpec(memory_space=pl.ANY)] * len(carried),
        out_specs=[s for s, _ in outs],
        out_shape=[s for _, s in outs],
        input_output_aliases=aliases,
        compiler_params=_params("arbitrary"),
        name="proj",
    )(h, g.reshape(g.shape[0], 1, d), w_in, w_gate, *carried)
    return ([r for k, r in enumerate(res) if k not in stacked_idx], [res[k] for k in stacked_idx])


def _proj_tail_kernel(h_ref, g_ref, wk_ref, wv_ref, kt_ref, vt_ref):
    u = _rms(h_ref[...], g_ref[...]).astype(BF16)
    kt_ref[...] = _dot(u, wk_ref[...]).T
    vt_ref[...] = _dot(u, wv_ref[...]).T


def _proj_tail(h3, g, w_in, layer, keep):
    b, t, d = h3.shape
    last = t // keep - 1
    wcol = lambda c: pl.BlockSpec((None, d, A_WIDTH), lambda i: (layer, 0, c), pipeline_mode=pl.Buffered(1))
    out = pl.BlockSpec((None, A_WIDTH, keep), lambda i: (i, 0, 0))
    return pl.pallas_call(
        _proj_tail_kernel,
        grid=(b,),
        in_specs=[pl.BlockSpec((None, keep, d), lambda i: (i, last, 0)), _layer((1, d), layer),
                  wcol(IN_KA // A_WIDTH), wcol(IN_VA // A_WIDTH)],
        out_specs=[out, out],
        out_shape=[jax.ShapeDtypeStruct((b, A_WIDTH, keep), F32)] * 2,
        compiler_params=_params("arbitrary"),
        name="proj_tail",
    )(h3, g.reshape(g.shape[0], 1, d), w_in, w_in)


def _band_bias_kernel(tab_ref, o_ref):
    qi = lax.broadcasted_iota(jnp.int32, (CHUNK, A_KEYS), 0)
    ki = lax.broadcasted_iota(jnp.int32, (CHUNK, A_KEYS), 1)
    idx = jnp.clip(A_WIN + qi - ki, -REL_CLIP, REL_CLIP) + REL_CLIP
    lo = REL_CLIP - CHUNK + 1
    hi = 2 * REL_CLIP

    for h in range(A_HEADS):
        g, hh = divmod(h, A_GROUP)
        o_ref[g, hh * CHUNK:(hh + 1) * CHUNK, :] = jnp.full((CHUNK, A_KEYS), tab_ref[h, hi], F32)

    def body(r, carry):
        hit = idx == r
        for h in range(A_HEADS):
            g, hh = divmod(h, A_GROUP)
            sl = (g, slice(hh * CHUNK, (hh + 1) * CHUNK), slice(None))
            o_ref[sl] = jnp.where(hit, tab_ref[h, r], o_ref[sl])
        return carry

    lax.fori_loop(lo, hi, body, 0)


def _band_bias(table):
    return pl.pallas_call(
        _band_bias_kernel,
        in_specs=[pl.BlockSpec(memory_space=pltpu.SMEM)],
        out_specs=pl.BlockSpec(memory_space=pltpu.VMEM),
        out_shape=jax.ShapeDtypeStruct((A_HEADS // A_GROUP, A_GROUP * CHUNK, A_KEYS), F32),
        name="band_bias",
    )(table)


def _block_diag_query(qg):
    lane_head = lax.broadcasted_iota(jnp.int32, qg.shape, 1) // HEAD_DIM
    return jnp.concatenate(
        [jnp.where(lane_head == hh, qg, jnp.zeros_like(qg)) for hh in range(A_GROUP)], axis=0)


def _gather_heads(o):
    lane_head = lax.broadcasted_iota(jnp.int32, (CHUNK, A_GROUP_WIDTH), 1) // HEAD_DIM
    og = jnp.zeros((CHUNK, A_GROUP_WIDTH), F32)
    for hh in range(A_GROUP):
        og = og + jnp.where(lane_head == hh, o[hh * CHUNK:(hh + 1) * CHUNK, :], 0.0)
    return og


def _band_prompt_kernel(nc, q_ref, k_ref, v_ref, bias_ref, o_ref, kpad_ref, vpad_ref):
    step = pl.program_id(1)
    t = k_ref.shape[0]

    @pl.when(step == 0)
    def _():
        kpad_ref[0:A_WIN, :] = jnp.zeros((A_WIN, A_WIDTH), BF16)
        vpad_ref[0:A_WIN, :] = jnp.zeros((A_WIN, A_WIDTH), BF16)
        kpad_ref[A_WIN:A_WIN + t, :] = k_ref[...]
        vpad_ref[A_WIN:A_WIN + t, :] = v_ref[...]

    for e in range(nc):
        c = step * nc + e
        start = pl.multiple_of(c * CHUNK, CHUNK)
        kw = kpad_ref[pl.ds(start, A_KEYS), :]
        vw = vpad_ref[pl.ds(start, A_KEYS), :]
        first_valid = (A_BAND - c) * CHUNK
        q = q_ref[e * CHUNK:(e + 1) * CHUNK, :]
        for g in range(A_HEADS // A_GROUP):
            cols = slice(g * A_GROUP_WIDTH, (g + 1) * A_GROUP_WIDTH)
            s = _dot_nt(_block_diag_query(q[:, cols]), kw[:, cols]) + bias_ref[g]
            kidx = lax.broadcasted_iota(jnp.int32, s.shape, 1)
            s = jnp.where(kidx >= first_valid, s, MASK_VALUE)
            m = jnp.max(s, axis=-1, keepdims=True)
            p = jnp.exp(s - m)
            l = jnp.sum(p, axis=-1, keepdims=True)
            o = _gather_heads(_dot(p.astype(BF16), vw[:, cols]) / l)
            o_ref[e * CHUNK:(e + 1) * CHUNK, cols] = o.astype(BF16)


def _band_prompt(qa, ka, va, bias, nc):
    b, t, _ = qa.shape
    rows = nc * CHUNK
    return pl.pallas_call(
        functools.partial(_band_prompt_kernel, nc),
        grid=(b, t // rows),
        in_specs=[pl.BlockSpec((None, rows, A_WIDTH), lambda i, c: (i, c, 0)),
                  pl.BlockSpec((None, t, A_WIDTH), lambda i, c: (i, 0, 0)),
                  pl.BlockSpec((None, t, A_WIDTH), lambda i, c: (i, 0, 0)),
                  _resident(bias.shape)],
        out_specs=pl.BlockSpec((None, rows, A_WIDTH), lambda i, c: (i, c, 0)),
        out_shape=jax.ShapeDtypeStruct((b, t, A_WIDTH), BF16),
        scratch_shapes=[pltpu.VMEM((A_WIN + t, A_WIDTH), BF16),
                        pltpu.VMEM((A_WIN + t, A_WIDTH), BF16)],
        compiler_params=_params("arbitrary", "arbitrary"),
        name="band_prompt",
    )(qa, ka, va, bias)


def _band_sample_kernel(q_ref, kct_ref, vct_ref, kn_ref, vn_ref, bias_ref, o_ref):
    q = q_ref[...]
    outs = []
    for g in range(A_HEADS // A_GROUP):
        cols = slice(g * A_GROUP_WIDTH, (g + 1) * A_GROUP_WIDTH)
        qbd = _block_diag_query(q[:, cols])
        s_old = _dot(qbd, kct_ref[cols, :].astype(BF16)) + bias_ref[g, :, 0:A_WIN]
        s_new = _dot_nt(qbd, kn_ref[:, cols]) + bias_ref[g, :, A_WIN:A_KEYS]
        m = jnp.maximum(jnp.max(s_old, axis=-1, keepdims=True), jnp.max(s_new, axis=-1, keepdims=True))
        p_old = jnp.exp(s_old - m)
        p_new = jnp.exp(s_new - m)
        l = jnp.sum(p_old, axis=-1, keepdims=True) + jnp.sum(p_new, axis=-1, keepdims=True)
        o = (_dot_nt(p_old.astype(BF16), vct_ref[cols, :].astype(BF16))
             + _dot(p_new.astype(BF16), vn_ref[:, cols]))
        outs.append(_gather_heads(o / l))
    o_ref[...] = jnp.concatenate(outs, axis=1).astype(BF16)


def _band_sample(qa, ka, va, cache_kt, cache_vt, bias, layer):
    b, t, _ = qa.shape
    w = cache_kt.shape[3]
    new = pl.BlockSpec((None, t, A_WIDTH), lambda i: (i, 0, 0))
    old = pl.BlockSpec((None, None, A_WIDTH, w), lambda i: (layer, i, 0, 0))
    return pl.pallas_call(
        _band_sample_kernel,
        grid=(b,),
        in_specs=[new, old, old, new, new, _resident(bias.shape)],
        out_specs=new,
        out_shape=jax.ShapeDtypeStruct((b, t, A_WIDTH), BF16),
        compiler_params=_params("arbitrary"),
        name="band_sample",
    )(qa, cache_kt, cache_vt, ka, va, bias)


def _diff_lambda(lq_ref, lam0):
    lq = lq_ref[...]
    a = jnp.sum(lq[0:1, :] * lq[1:2, :], axis=-1, keepdims=True)
    b = jnp.sum(lq[2:3, :] * lq[3:4, :], axis=-1, keepdims=True)
    return jnp.exp(a) - jnp.exp(b) + lam0


def _stack_components(q):
    lane = lax.broadcasted_iota(jnp.int32, q.shape, 1)
    zero = jnp.zeros_like(q)
    return jnp.concatenate([jnp.where(lane < HEAD_DIM, q, zero),
                            jnp.where(lane >= HEAD_DIM, q, zero)], axis=0)


def _own_block_bias_and_mask(slope, qpos, kpos):
    bias = slope * jnp.minimum(kpos, 2 * qpos - kpos).astype(F32)
    valid = (kpos // CHUNK) <= (qpos // CHUNK)
    return bias, valid


def _diff_prompt_kernel(lam0, hb, tq, tk, slopes_ref, lq_ref, subg_ref, q_ref, k_ref, vt_ref, o_ref,
                        acc_ref):
    hg = pl.program_id(1)
    qi = pl.program_id(2)
    q_start = qi * tq
    tv = vt_ref.shape[2]
    lane_reps = 2 * tq // LANES
    key_iota = lax.broadcasted_iota(jnp.int32, (tk, LANES), 0)
    ones_rows = jnp.ones((SUM_ROWS, tv), BF16)
    head_cols = [slice(hh * B_HEAD_WIDTH, (hh + 1) * B_HEAD_WIDTH) for hh in range(hb)]
    slopes = [slopes_ref[hg * hb + hh] * LOG2E for hh in range(hb)]
    qs = [_stack_components(q_ref[:, c]) for c in head_cols]

    acc_ref[...] = jnp.zeros(acc_ref.shape, F32)

    def update(hh, s, first_v_block, m_old):
        m_new = jnp.maximum(m_old, jnp.max(s, axis=0, keepdims=True))
        alpha = jnp.exp2(m_old - m_new)
        pb = jnp.exp2(s - m_new).astype(BF16)
        pv = None
        for u in range(tk // tv):
            vt = jnp.concatenate([vt_ref[first_v_block + u, head_cols[hh], :], ones_rows], axis=0)
            part = _dot(vt, pb[u * tv:(u + 1) * tv])
            pv = part if pv is None else pv + part
        acc_ref[hh] = alpha * acc_ref[hh] + pv
        return m_new

    def below(j, ms):
        ks = pl.multiple_of(j * tk, tk)
        out = []
        for hh in range(hb):
            kbias = slopes[hh] * (key_iota + (ks - q_start)).astype(F32)
            s = _dot_nt(k_ref[pl.ds(ks, tk), head_cols[hh]], qs[hh]) + jnp.tile(kbias, (1, lane_reps))
            out.append(update(hh, s, j * (tk // tv), ms[hh]))
        return tuple(out)

    m0 = jnp.full((1, 2 * tq), MASK_VALUE, F32)
    ms = lax.fori_loop(0, q_start // tk, below, (m0,) * hb)

    for d in range(tq // tk):
        ks = pl.multiple_of(q_start + d * tk, tk)
        kpos = lax.broadcasted_iota(jnp.int32, (tk, tq), 0) + d * tk
        qpos = lax.broadcasted_iota(jnp.int32, (tk, tq), 1)
        out = []
        for hh in range(hb):
            bias, valid = _own_block_bias_and_mask(slopes[hh], qpos, kpos)
            bias = jnp.concatenate([bias, bias], axis=1)
            valid = jnp.concatenate([valid, valid], axis=1)
            s = _dot_nt(k_ref[pl.ds(ks, tk), head_cols[hh]], qs[hh]) + bias
            out.append(update(hh, jnp.where(valid, s, MASK_VALUE), (q_start + d * tk) // tv, ms[hh]))
        ms = tuple(out)

    lam = _diff_lambda(lq_ref, lam0)
    for hh in range(hb):
        acc = acc_ref[hh]
        num = acc[0:B_HEAD_WIDTH]
        l = acc[B_HEAD_WIDTH:B_HEAD_WIDTH + 1]
        ot = num[:, :tq] / l[:, :tq] - lam * (num[:, tq:] / l[:, tq:])
        ot = ot * lax.rsqrt(jnp.mean(ot * ot, axis=0, keepdims=True) + NORM_EPS)
        o_ref[:, head_cols[hh]] = (ot.T * subg_ref[...] * (1.0 - lam0)).astype(BF16)


def _diff_prompt(qb, kb, vbt, lq, subg, lam0, layer, hb, tq, tk):
    b, t, _ = qb.shape
    nblk, tv = vbt.shape[1], vbt.shape[3]
    assert tk % tv == 0 and tq % tk == 0 and B_HEADS % hb == 0
    w = hb * B_HEAD_WIDTH
    blk = pl.BlockSpec((None, tq, w), lambda i, h, q: (i, q, h))
    return pl.pallas_call(
        functools.partial(_diff_prompt_kernel, lam0, hb, tq, tk),
        grid=(b, B_HEADS // hb, t // tq),
        in_specs=[pl.BlockSpec(memory_space=pltpu.SMEM), _layer(lq.shape[1:], layer),
                  _layer((1, B_HEAD_WIDTH), layer), blk,
                  pl.BlockSpec((None, t, w), lambda i, h, q: (i, 0, h)),
                  pl.BlockSpec((None, nblk, w, tv), lambda i, h, q: (i, 0, h, 0))],
        out_specs=blk,
        out_shape=jax.ShapeDtypeStruct((b, t, B_WIDTH), BF16),
        scratch_shapes=[pltpu.VMEM((hb, B_HEAD_WIDTH + SUM_ROWS, 2 * tq), F32)],
        compiler_params=_params("arbitrary", "arbitrary", "arbitrary"),
        name="diff_prompt",
    )(_alibi_slopes(B_HEADS), lq, subg.reshape(subg.shape[0], 1, B_HEAD_WIDTH), qb, kb, vbt)


def _diff_sample_kernel(lam0, nkc, slopes_ref, lq_ref, subg_ref, q_ref, kct_ref, vch_ref, kn_ref,
                        vn_ref, o_ref, m_ref, l_ref, acc_ref):
    kc = pl.program_id(1)
    t = q_ref.shape[0]
    tc = kct_ref.shape[1]
    past = tc * nkc

    @pl.when(kc == 0)
    def _():
        m_ref[...] = jnp.full(m_ref.shape, MASK_VALUE, F32)
        l_ref[...] = jnp.zeros(l_ref.shape, F32)
        acc_ref[...] = jnp.zeros(acc_ref.shape, F32)

    def update(h, s, v):
        width = s.shape[1]
        m_old = m_ref[h]
        m_new = jnp.maximum(m_old, jnp.max(s, axis=1, keepdims=True))
        alpha = jnp.exp2(m_old - m_new)
        if width % LANES == 0:
            p = jnp.exp2(s - jnp.tile(m_new, (1, width // LANES)))
        else:
            p = jnp.exp2(s - m_new[:, :width])
        v_and_ones = jnp.concatenate([v, jnp.ones(v.shape, BF16)], axis=1)
        pv = _dot(p.astype(BF16), v_and_ones)
        l_ref[h] = alpha * l_ref[h] + pv[:, B_HEAD_WIDTH:]
        acc_ref[h] = alpha * acc_ref[h] + pv[:, :B_HEAD_WIDTH]
        m_ref[h] = m_new

    kpos = (kc * tc - past + lax.broadcasted_iota(jnp.int32, (1, tc), 1)).astype(F32)
    for h in range(B_HEADS):
        cols = slice(h * B_HEAD_WIDTH, (h + 1) * B_HEAD_WIDTH)
        qs = _stack_components(q_ref[:, cols])
        s = _dot(qs, kct_ref[cols, :].astype(BF16)) + (slopes_ref[h] * LOG2E) * kpos
        update(h, s, vch_ref[pl.ds(h, tc, stride=B_HEADS), :].astype(BF16))

    @pl.when(kc == nkc - 1)
    def _():
        lam = _diff_lambda(lq_ref, lam0)
        qpos = lax.broadcasted_iota(jnp.int32, (2 * t, t), 0) % t
        kidx = lax.broadcasted_iota(jnp.int32, (2 * t, t), 1)
        for h in range(B_HEADS):
            cols = slice(h * B_HEAD_WIDTH, (h + 1) * B_HEAD_WIDTH)
            qs = _stack_components(q_ref[:, cols])
            bias, _ = _own_block_bias_and_mask(slopes_ref[h] * LOG2E, qpos, kidx)
            update(h, _dot_nt(qs, kn_ref[:, cols]) + bias, vn_ref[:, cols])
            acc = acc_ref[h]
            l = l_ref[h]
            o = acc[:t] / l[:t] - lam * (acc[t:] / l[t:])
            o_ref[:, cols] = (_rms(o, subg_ref[...]) * (1.0 - lam0)).astype(BF16)


def _diff_sample(qb, kb, vb, cache_kt, cache_vh, lq, subg, lam0, layer, tc):
    b, t, _ = qb.shape
    past = cache_kt.shape[3]
    new = pl.BlockSpec((None, t, B_WIDTH), lambda i, k: (i, 0, 0))
    stat = pltpu.VMEM((B_HEADS, 2 * t, LANES), F32)
    return pl.pallas_call(
        functools.partial(_diff_sample_kernel, lam0, past // tc),
        grid=(b, past // tc),
        in_specs=[pl.BlockSpec(memory_space=pltpu.SMEM), _layer(lq.shape[1:], layer),
                  _layer((1, B_HEAD_WIDTH), layer), new,
                  pl.BlockSpec((None, None, B_WIDTH, tc), lambda i, k: (layer, i, 0, k)),
                  pl.BlockSpec((None, None, tc * B_HEADS, B_HEAD_WIDTH), lambda i, k: (layer, i, k, 0)),
                  new, new],
        out_specs=new,
        out_shape=jax.ShapeDtypeStruct((b, t, B_WIDTH), BF16),
        scratch_shapes=[stat, stat, pltpu.VMEM((B_HEADS, 2 * t, B_HEAD_WIDTH), F32)],
        compiler_params=_params("arbitrary", "arbitrary"),
        name="diff_sample",
    )(_alibi_slopes(B_HEADS), lq, subg.reshape(subg.shape[0], 1, B_HEAD_WIDTH), qb, cache_kt, cache_vh,
      kb, vb)


def _merge_kernel(ff_chunk, h_ref, gate_ref, oa_ref, ob_ref, ple_ref,
                  wba_ref, wbb_ref, wout_ref, wup_ref, wdown_ref, wple_ref, wpg_ref,
                  g_post_ref, g_fpre_ref, g_fpost_ref, g_ple_ref, o_ref):
    d = h_ref.shape[1]
    gate = gate_ref[...]
    mix = (gate[:, :d].astype(F32) * _dot(oa_ref[...], wba_ref[...])
           + gate[:, d:].astype(F32) * _dot(ob_ref[...], wbb_ref[...]))
    h = h_ref[...] + _rms(_dot(mix.astype(BF16), wout_ref[...]), g_post_ref[...])

    u = _rms(h, g_fpre_ref[...]).astype(BF16)
    f = jnp.zeros(h.shape, F32)
    for lo in range(0, wup_ref.shape[1], ff_chunk):
        act = jnp.square(jnp.maximum(_dot(u, wup_ref[:, lo:lo + ff_chunk]), 0.0))
        f = f + _dot(act.astype(BF16), wdown_ref[lo:lo + ff_chunk, :])
    h = h + _rms(f, g_fpost_ref[...])

    pg = jax.nn.sigmoid(_dot(h.astype(BF16), wpg_ref[...]))
    pe = _rms(_dot(ple_ref[...].astype(BF16), wple_ref[...]), g_ple_ref[...])
    o_ref[...] = h + pg * pe


def _merge(h, gate, oa, ob, ple, w, gains, layer, tm, ff_chunk):
    n, d = h.shape
    row = lambda a: pl.BlockSpec((tm, a.shape[1]), lambda i: (i, 0))
    acts = [h, gate, oa, ob]
    gains = [g.reshape(g.shape[0], 1, d) for g in gains]
    return pl.pallas_call(
        functools.partial(_merge_kernel, ff_chunk),
        grid=(n // tm,),
        in_specs=([row(a) for a in acts]
                  + [pl.BlockSpec((None, tm, ple.shape[2]), lambda i: (layer, i, 0))]
                  + [_layer(x.shape[1:], layer) for x in w] + [_layer((1, d), layer)] * 4),
        out_specs=pl.BlockSpec((tm, d), lambda i: (i, 0)),
        out_shape=jax.ShapeDtypeStruct((n, d), F32),
        compiler_params=_params("arbitrary"),
        name="merge",
    )(*acts, ple, *w, *gains)


ROW_TILE = 512
MERGE_TILE = 512
FF_CHUNK = 1024
BAND_CHUNKS = 4
DIFF_HEADS = 4
DIFF_TQ = 512
DIFF_TK = 512
DIFF_SAMPLE_TC = 2048


def kernel(x_prompt, x_sample, cache_a_k, cache_a_v, cache_b_k, cache_b_v, p_prompt, p_sample,
           w_in, rel_bias, diff_lambda, diff_subln, w_gate, w_branch_a, w_branch_b, w_out,
           g_mix_pre, g_mix_post, g_ffn_pre, g_ffn_post, w_up, w_down, w_ple, g_ple, w_ple_gate):
    depth = w_in.shape[0]
    bp, tp, d = x_prompt.shape
    bs, ts, _ = x_sample.shape
    past = cache_b_k.shape[2]
    a_win = cache_a_k.shape[2]
    keep = min(A_WIN, tp)
    assert a_win == A_WIN and keep == A_WIN and ts == CHUNK and tp % DIFF_TQ == 0

    w_in_b, w_gate_b = w_in.astype(BF16), w_gate.astype(BF16)
    merge_w = tuple(x.astype(BF16) for x in (w_branch_a, w_branch_b, w_out, w_up, w_down, w_ple,
                                             w_ple_gate))
    gains = (g_mix_post, g_ffn_pre, g_ffn_post, g_ple)

    cache_a_kt = jnp.transpose(cache_a_k, (0, 1, 3, 4, 2)).reshape(depth, bs, A_WIDTH, a_win)
    cache_a_vt = jnp.transpose(cache_a_v, (0, 1, 3, 4, 2)).reshape(depth, bs, A_WIDTH, a_win)
    cache_b_kt = jnp.transpose(cache_b_k, (0, 1, 3, 4, 5, 2)).reshape(depth, bs, B_WIDTH, past)
    cache_b_vh = cache_b_v.reshape(depth, bs, past * B_HEADS, B_HEAD_WIDTH)
    ple_p = p_prompt.reshape(depth, bp * tp, -1)
    ple_s = p_sample.reshape(depth, bs * ts, -1)

    hp = x_prompt.reshape(bp * tp, d)
    hs = x_sample.reshape(bs * ts, d)
    tails = ([], [])
    stacked_p = stacked_s = None

    for i in range(depth):
        lam0 = _lambda_init(i)
        bias = _band_bias(rel_bias[i])

        akt, avt = _proj_tail(hp.reshape(bp, tp, d), g_mix_pre, w_in_b, i, keep)
        tails[0].append(akt)
        tails[1].append(avt)
        (qa, kab, vab, qb, kbb, vbt, gate), stacked_p = _proj(hp, g_mix_pre, w_in_b, w_gate_b, i,
                                                              ROW_TILE, tp, stacked_p)
        r3 = lambda a: a.reshape(bp, tp, a.shape[-1])
        oa = _band_prompt(r3(qa), r3(kab), r3(vab), bias, BAND_CHUNKS)
        ob = _diff_prompt(r3(qb), r3(kbb), vbt, diff_lambda, diff_subln, lam0, i, DIFF_HEADS, DIFF_TQ,
                          DIFF_TK)
        hp = _merge(hp, gate, oa.reshape(bp * tp, -1), ob.reshape(bp * tp, -1), ple_p, merge_w, gains,
                    i, MERGE_TILE, FF_CHUNK)

        (qa, kab, vab, qb, kbb, vbb, gate), stacked_s = _proj(hs, g_mix_pre, w_in_b, w_gate_b, i,
                                                              ROW_TILE, None, stacked_s)
        r3 = lambda a: a.reshape(bs, ts, a.shape[-1])
        oa = _band_sample(r3(qa), r3(kab), r3(vab), cache_a_kt, cache_a_vt, bias, i)
        ob = _diff_sample(r3(qb), r3(kbb), r3(vbb), cache_b_kt, cache_b_vh, diff_lambda, diff_subln,
                          lam0, i, DIFF_SAMPLE_TC)
        hs = _merge(hs, gate, oa.reshape(bs * ts, -1), ob.reshape(bs * ts, -1), ple_s, merge_w, gains,
                    i, MERGE_TILE, FF_CHUNK)

    kbt_p, vbh_p = stacked_p
    kaf_s, vaf_s, kbf_s, vbh_s = stacked_s
    a_feature_major = lambda a: jnp.transpose(
        jnp.stack(a).reshape(depth, bp, A_HEADS, HEAD_DIM, keep), (0, 1, 4, 2, 3))
    return (hp.reshape(bp, tp, d), hs.reshape(bs, ts, d),
            a_feature_major(tails[0]), a_feature_major(tails[1]),
            jnp.transpose(kbt_p.reshape(depth, bp, B_HEADS, 2, HEAD_DIM, tp), (0, 1, 5, 2, 3, 4)),
            vbh_p.reshape(depth, bp, tp, B_HEADS, B_HEAD_WIDTH),
            kaf_s.reshape(depth, bs, ts, A_HEADS, HEAD_DIM), vaf_s.reshape(depth, bs, ts, A_HEADS, HEAD_DIM),
            kbf_s.reshape(depth, bs, ts, B_HEADS, 2, HEAD_DIM),
            vbh_s.reshape(depth, bs, ts, B_HEADS, B_HEAD_WIDTH))
```

```python
import functools
import math

import jax
import jax.numpy as jnp
import numpy as np
from jax import lax
from jax.experimental import pallas as pl
from jax.experimental.pallas import tpu as pltpu

CHUNK = 64
HEAD_DIM = 64
A_HEADS = 8
A_BAND = 8
A_WIDTH = A_HEADS * HEAD_DIM
A_WIN = A_BAND * CHUNK
A_KEYS = A_WIN + CHUNK
REL_CLIP = 128
B_HEADS = 8
B_HEAD_WIDTH = 2 * HEAD_DIM
B_WIDTH = B_HEADS * B_HEAD_WIDTH
NORM_EPS = 1e-6
MASK_VALUE = -1e30
QK_SCALE = HEAD_DIM ** -0.5
LOG2E = math.log2(math.e)
SUM_ROWS = 16

A_GROUP = 4
A_GROUP_WIDTH = A_GROUP * HEAD_DIM
LANES = 128

V7X_VMEM_LIMIT_BYTES = 60000 * 1024

BF16 = jnp.bfloat16
F32 = jnp.float32


def _lambda_init(layer):
    return 0.8 - 0.6 * math.exp(-0.3 * layer)


def _alibi_slopes(n):
    return jnp.asarray(2.0 ** (-8.0 * np.arange(1, n + 1) / n), dtype=F32)


def _rms(x, g):
    return x * lax.rsqrt(jnp.mean(x * x, axis=-1, keepdims=True) + NORM_EPS) * g


def _dot(a, b):
    return jnp.dot(a, b, preferred_element_type=F32)


def _dot_nt(a, b):
    return lax.dot_general(a, b, (((1,), (1,)), ((), ())), preferred_element_type=F32)


def _resident(shape):
    nd = len(shape)
    return pl.BlockSpec(shape, lambda *_: (0,) * nd, pipeline_mode=pl.Buffered(1))


def _layer(shape, layer):
    nd = len(shape)
    return pl.BlockSpec((None,) + tuple(shape), lambda *_: (layer,) + (0,) * nd,
                        pipeline_mode=pl.Buffered(1))


def _params(*sem):
    return pltpu.CompilerParams(dimension_semantics=sem, vmem_limit_bytes=V7X_VMEM_LIMIT_BYTES)


IN_QA, IN_KA, IN_VA = 0, A_WIDTH, 2 * A_WIDTH
IN_QB, IN_KB, IN_VB = 3 * A_WIDTH, 3 * A_WIDTH + B_WIDTH, 3 * A_WIDTH + 2 * B_WIDTH


def _store_head_tiled(o_ref, x):
    rows = x.shape[0]
    for h in range(B_HEADS):
        o_ref[pl.ds(h, rows, stride=B_HEADS), :] = x[:, h * B_HEAD_WIDTH:(h + 1) * B_HEAD_WIDTH]


def _proj_kernel(prompt, n_carried, layer, h_ref, g_ref, win_ref, wg_ref, *refs):
    out_refs = refs[n_carried:]

    def stacked(ref):
        if n_carried:
            return ref
        for other in range(ref.shape[0]):
            if other != layer:
                ref[other] = jnp.zeros(ref.shape[1:], ref.dtype)
        return ref.at[layer]

    u = _rms(h_ref[...], g_ref[...]).astype(BF16)

    def col(lo, width):
        return _dot(u, win_ref[:, lo:lo + width])

    qa = (col(IN_QA, A_WIDTH) * QK_SCALE).astype(BF16)
    ka = col(IN_KA, A_WIDTH)
    va = col(IN_VA, A_WIDTH)
    qb = (col(IN_QB, B_WIDTH) * (QK_SCALE * LOG2E)).astype(BF16)
    kb = col(IN_KB, B_WIDTH)
    vb = col(IN_VB, B_WIDTH)
    gate = jax.nn.sigmoid(_dot(u, wg_ref[...])).astype(BF16)
    if prompt:
        qa_ref, kab_ref, vab_ref, qb_ref, kbb_ref, kbt_ref, vbt_ref, vbh_ref, gate_ref = out_refs
        stacked(kbt_ref)[...] = kb.T
        vbt_ref[...] = vb.T.astype(BF16)
    else:
        (qa_ref, kab_ref, vab_ref, qb_ref, kbb_ref, kaf_ref, vaf_ref, kbf_ref, vbb_ref, vbh_ref,
         gate_ref) = out_refs
        stacked(kaf_ref)[...] = ka
        stacked(vaf_ref)[...] = va
        stacked(kbf_ref)[...] = kb
        vbb_ref[...] = vb.astype(BF16)
    qa_ref[...] = qa
    kab_ref[...] = ka.astype(BF16)
    vab_ref[...] = va.astype(BF16)
    qb_ref[...] = qb
    kbb_ref[...] = kb.astype(BF16)
    _store_head_tiled(stacked(vbh_ref), vb)
    gate_ref[...] = gate


def _proj(h, g, w_in, w_gate, layer, tm, prompt_seq, carried):
    n, d = h.shape
    depth, _, gate_w = w_gate.shape
    first = carried is None
    row = lambda w: pl.BlockSpec((tm, w), lambda i: (i, 0))
    sds = jax.ShapeDtypeStruct
    tok = lambda w, dt: (row(w), sds((n, w), dt))

    def layer_block(block, index_map):
        if first:
            return pl.BlockSpec((depth,) + block, lambda i: (0,) + index_map(i))
        return pl.BlockSpec((None,) + block, lambda i: (layer,) + index_map(i))

    tok_stacked = lambda w: (layer_block((tm, w), lambda i: (i, 0)), sds((depth, n, w), F32))
    head_tiled = (layer_block((tm * B_HEADS, B_HEAD_WIDTH), lambda i: (i, 0)),
                  sds((depth, n * B_HEADS, B_HEAD_WIDTH), F32))
    local = [tok(A_WIDTH, BF16), tok(A_WIDTH, BF16), tok(A_WIDTH, BF16), tok(B_WIDTH, BF16),
             tok(B_WIDTH, BF16)]
    if prompt_seq is not None:
        nblk = prompt_seq // tm
        b = n // prompt_seq
        outs = local + [
            (layer_block((None, B_WIDTH, tm), lambda i: (i // nblk, 0, i % nblk)),
             sds((depth, b, B_WIDTH, prompt_seq), F32)),
            (pl.BlockSpec((None, None, B_WIDTH, tm), lambda i: (i // nblk, i % nblk, 0, 0)),
             sds((b, nblk, B_WIDTH, tm), BF16)),
            head_tiled, tok(gate_w, BF16)]
        stacked_idx = [5, 7]
    else:
        outs = local + [tok_stacked(A_WIDTH), tok_stacked(A_WIDTH), tok_stacked(B_WIDTH),
                        tok(B_WIDTH, BF16), head_tiled, tok(gate_w, BF16)]
        stacked_idx = [5, 6, 7, 9]
    n_fixed_inputs = 4
    carried = [] if carried is None else list(carried)
    aliases = {n_fixed_inputs + k: stacked_idx[k] for k in range(len(carried))}
    res = pl.pallas_call(
        functools.partial(_proj_kernel, prompt_seq is not None, len(carried), layer),
        grid=(n // tm,),
        in_specs=[row(d), _layer((1, d), layer), _layer(w_in.shape[1:], layer),
                  _layer(w_gate.shape[1:], layer)] + [pl.BlockSpec(memory_space=pl.ANY)] * len(carried),
        out_specs=[s for s, _ in outs],
        out_shape=[s for _, s in outs],
        input_output_aliases=aliases,
        compiler_params=_params("arbitrary"),
        name="proj",
    )(h, g.reshape(g.shape[0], 1, d), w_in, w_gate, *carried)
    return ([r for k, r in enumerate(res) if k not in stacked_idx], [res[k] for k in stacked_idx])


def _proj_tail_kernel(h_ref, g_ref, wk_ref, wv_ref, kt_ref, vt_ref):
    u = _rms(h_ref[...], g_ref[...]).astype(BF16)
    kt_ref[...] = _dot(u, wk_ref[...]).T
    vt_ref[...] = _dot(u, wv_ref[...]).T


def _proj_tail(h3, g, w_in, layer, keep):
    b, t, d = h3.shape
    last = t // keep - 1
    wcol = lambda c: pl.BlockSpec((None, d, A_WIDTH), lambda i: (layer, 0, c), pipeline_mode=pl.Buffered(1))
    out = pl.BlockSpec((None, A_WIDTH, keep), lambda i: (i, 0, 0))
    return pl.pallas_call(
        _proj_tail_kernel,
        grid=(b,),
        in_specs=[pl.BlockSpec((None, keep, d), lambda i: (i, last, 0)), _layer((1, d), layer),
                  wcol(IN_KA // A_WIDTH), wcol(IN_VA // A_WIDTH)],
        out_specs=[out, out],
        out_shape=[jax.ShapeDtypeStruct((b, A_WIDTH, keep), F32)] * 2,
        compiler_params=_params("arbitrary"),
        name="proj_tail",
    )(h3, g.reshape(g.shape[0], 1, d), w_in, w_in)


def _band_bias_kernel(tab_ref, o_ref):
    qi = lax.broadcasted_iota(jnp.int32, (CHUNK, A_KEYS), 0)
    ki = lax.broadcasted_iota(jnp.int32, (CHUNK, A_KEYS), 1)
    idx = jnp.clip(A_WIN + qi - ki, -REL_CLIP, REL_CLIP) + REL_CLIP
    lo = REL_CLIP - CHUNK + 1
    hi = 2 * REL_CLIP

    for h in range(A_HEADS):
        g, hh = divmod(h, A_GROUP)
        o_ref[g, hh * CHUNK:(hh + 1) * CHUNK, :] = jnp.full((CHUNK, A_KEYS), tab_ref[h, hi], F32)

    def body(r, carry):
        hit = idx == r
        for h in range(A_HEADS):
            g, hh = divmod(h, A_GROUP)
            sl = (g, slice(hh * CHUNK, (hh + 1) * CHUNK), slice(None))
            o_ref[sl] = jnp.where(hit, tab_ref[h, r], o_ref[sl])
        return carry

    lax.fori_loop(lo, hi, body, 0)


def _band_bias(table):
    return pl.pallas_call(
        _band_bias_kernel,
        in_specs=[pl.BlockSpec(memory_space=pltpu.SMEM)],
        out_specs=pl.BlockSpec(memory_space=pltpu.VMEM),
        out_shape=jax.ShapeDtypeStruct((A_HEADS // A_GROUP, A_GROUP * CHUNK, A_KEYS), F32),
        name="band_bias",
    )(table)


def _block_diag_query(qg):
    lane_head = lax.broadcasted_iota(jnp.int32, qg.shape, 1) // HEAD_DIM
    return jnp.concatenate(
        [jnp.where(lane_head == hh, qg, jnp.zeros_like(qg)) for hh in range(A_GROUP)], axis=0)


def _gather_heads(o):
    lane_head = lax.broadcasted_iota(jnp.int32, (CHUNK, A_GROUP_WIDTH), 1) // HEAD_DIM
    og = jnp.zeros((CHUNK, A_GROUP_WIDTH), F32)
    for hh in range(A_GROUP):
        og = og + jnp.where(lane_head == hh, o[hh * CHUNK:(hh + 1) * CHUNK, :], 0.0)
    return og


def _band_prompt_kernel(nc, q_ref, k_ref, v_ref, bias_ref, o_ref, kpad_ref, vpad_ref):
    step = pl.program_id(1)
    t = k_ref.shape[0]

    @pl.when(step == 0)
    def _():
        kpad_ref[0:A_WIN, :] = jnp.zeros((A_WIN, A_WIDTH), BF16)
        vpad_ref[0:A_WIN, :] = jnp.zeros((A_WIN, A_WIDTH), BF16)
        kpad_ref[A_WIN:A_WIN + t, :] = k_ref[...]
        vpad_ref[A_WIN:A_WIN + t, :] = v_ref[...]

    for e in range(nc):
        c = step * nc + e
        start = pl.multiple_of(c * CHUNK, CHUNK)
        kw = kpad_ref[pl.ds(start, A_KEYS), :]
        vw = vpad_ref[pl.ds(start, A_KEYS), :]
        first_valid = (A_BAND - c) * CHUNK
        q = q_ref[e * CHUNK:(e + 1) * CHUNK, :]
        for g in range(A_HEADS // A_GROUP):
            cols = slice(g * A_GROUP_WIDTH, (g + 1) * A_GROUP_WIDTH)
            s = _dot_nt(_block_diag_query(q[:, cols]), kw[:, cols]) + bias_ref[g]
            kidx = lax.broadcasted_iota(jnp.int32, s.shape, 1)
            s = jnp.where(kidx >= first_valid, s, MASK_VALUE)
            m = jnp.max(s, axis=-1, keepdims=True)
            p = jnp.exp(s - m)
            l = jnp.sum(p, axis=-1, keepdims=True)
            o = _gather_heads(_dot(p.astype(BF16), vw[:, cols]) / l)
            o_ref[e * CHUNK:(e + 1) * CHUNK, cols] = o.astype(BF16)


def _band_prompt(qa, ka, va, bias, nc):
    b, t, _ = qa.shape
    rows = nc * CHUNK
    return pl.pallas_call(
        functools.partial(_band_prompt_kernel, nc),
        grid=(b, t // rows),
        in_specs=[pl.BlockSpec((None, rows, A_WIDTH), lambda i, c: (i, c, 0)),
                  pl.BlockSpec((None, t, A_WIDTH), lambda i, c: (i, 0, 0)),
                  pl.BlockSpec((None, t, A_WIDTH), lambda i, c: (i, 0, 0)),
                  _resident(bias.shape)],
        out_specs=pl.BlockSpec((None, rows, A_WIDTH), lambda i, c: (i, c, 0)),
        out_shape=jax.ShapeDtypeStruct((b, t, A_WIDTH), BF16),
        scratch_shapes=[pltpu.VMEM((A_WIN + t, A_WIDTH), BF16),
                        pltpu.VMEM((A_WIN + t, A_WIDTH), BF16)],
        compiler_params=_params("arbitrary", "arbitrary"),
        name="band_prompt",
    )(qa, ka, va, bias)


def _band_sample_kernel(q_ref, kct_ref, vct_ref, kn_ref, vn_ref, bias_ref, o_ref):
    q = q_ref[...]
    outs = []
    for g in range(A_HEADS // A_GROUP):
        cols = slice(g * A_GROUP_WIDTH, (g + 1) * A_GROUP_WIDTH)
        qbd = _block_diag_query(q[:, cols])
        s_old = _dot(qbd, kct_ref[cols, :].astype(BF16)) + bias_ref[g, :, 0:A_WIN]
        s_new = _dot_nt(qbd, kn_ref[:, cols]) + bias_ref[g, :, A_WIN:A_KEYS]
        m = jnp.maximum(jnp.max(s_old, axis=-1, keepdims=True), jnp.max(s_new, axis=-1, keepdims=True))
        p_old = jnp.exp(s_old - m)
        p_new = jnp.exp(s_new - m)
        l = jnp.sum(p_old, axis=-1, keepdims=True) + jnp.sum(p_new, axis=-1, keepdims=True)
        o = (_dot_nt(p_old.astype(BF16), vct_ref[cols, :].astype(BF16))
             + _dot(p_new.astype(BF16), vn_ref[:, cols]))
        outs.append(_gather_heads(o / l))
    o_ref[...] = jnp.concatenate(outs, axis=1).astype(BF16)


def _band_sample(qa, ka, va, cache_kt, cache_vt, bias, layer):
    b, t, _ = qa.shape
    w = cache_kt.shape[3]
    new = pl.BlockSpec((None, t, A_WIDTH), lambda i: (i, 0, 0))
    old = pl.BlockSpec((None, None, A_WIDTH, w), lambda i: (layer, i, 0, 0))
    return pl.pallas_call(
        _band_sample_kernel,
        grid=(b,),
        in_specs=[new, old, old, new, new, _resident(bias.shape)],
        out_specs=new,
        out_shape=jax.ShapeDtypeStruct((b, t, A_WIDTH), BF16),
        compiler_params=_params("arbitrary"),
        name="band_sample",
    )(qa, cache_kt, cache_vt, ka, va, bias)


def _diff_lambda(lq_ref, lam0):
    lq = lq_ref[...]
    a = jnp.sum(lq[0:1, :] * lq[1:2, :], axis=-1, keepdims=True)
    b = jnp.sum(lq[2:3, :] * lq[3:4, :], axis=-1, keepdims=True)
    return jnp.exp(a) - jnp.exp(b) + lam0


def _stack_components(q):
    lane = lax.broadcasted_iota(jnp.int32, q.shape, 1)
    zero = jnp.zeros_like(q)
    return jnp.concatenate([jnp.where(lane < HEAD_DIM, q, zero),
                            jnp.where(lane >= HEAD_DIM, q, zero)], axis=0)


def _own_block_bias_and_mask(slope, qpos, kpos):
    bias = slope * jnp.minimum(kpos, 2 * qpos - kpos).astype(F32)
    valid = (kpos // CHUNK) <= (qpos // CHUNK)
    return bias, valid


def _diff_prompt_kernel(lam0, hb, tq, tk, slopes_ref, lq_ref, subg_ref, q_ref, k_ref, vt_ref, o_ref,
                        acc_ref):
    hg = pl.program_id(1)
    qi = pl.program_id(2)
    q_start = qi * tq
    tv = vt_ref.shape[2]
    lane_reps = 2 * tq // LANES
    key_iota = lax.broadcasted_iota(jnp.int32, (tk, LANES), 0)
    ones_rows = jnp.ones((SUM_ROWS, tv), BF16)
    head_cols = [slice(hh * B_HEAD_WIDTH, (hh + 1) * B_HEAD_WIDTH) for hh in range(hb)]
    slopes = [slopes_ref[hg * hb + hh] * LOG2E for hh in range(hb)]
    qs = [_stack_components(q_ref[:, c]) for c in head_cols]

    acc_ref[...] = jnp.zeros(acc_ref.shape, F32)

    def update(hh, s, first_v_block, m_old):
        m_new = jnp.maximum(m_old, jnp.max(s, axis=0, keepdims=True))
        alpha = jnp.exp2(m_old - m_new)
        pb = jnp.exp2(s - m_new).astype(BF16)
        pv = None
        for u in range(tk // tv):
            vt = jnp.concatenate([vt_ref[first_v_block + u, head_cols[hh], :], ones_rows], axis=0)
            part = _dot(vt, pb[u * tv:(u + 1) * tv])
            pv = part if pv is None else pv + part
        acc_ref[hh] = alpha * acc_ref[hh] + pv
        return m_new

    def below(j, ms):
        ks = pl.multiple_of(j * tk, tk)
        scores = []
        for hh in range(hb):
            kbias = slopes[hh] * (key_iota + (ks - q_start)).astype(F32)
            scores.append(_dot_nt(k_ref[pl.ds(ks, tk), head_cols[hh]], qs[hh])
                          + jnp.tile(kbias, (1, lane_reps)))
        return tuple(update(hh, scores[hh], j * (tk // tv), ms[hh]) for hh in range(hb))

    m0 = jnp.full((1, 2 * tq), MASK_VALUE, F32)
    ms = lax.fori_loop(0, q_start // tk, below, (m0,) * hb)

    for d in range(tq // tk):
        ks = pl.multiple_of(q_start + d * tk, tk)
        kpos = lax.broadcasted_iota(jnp.int32, (tk, tq), 0) + d * tk
        qpos = lax.broadcasted_iota(jnp.int32, (tk, tq), 1)
        scores = []
        for hh in range(hb):
            bias, valid = _own_block_bias_and_mask(slopes[hh], qpos, kpos)
            bias = jnp.concatenate([bias, bias], axis=1)
            valid = jnp.concatenate([valid, valid], axis=1)
            s = _dot_nt(k_ref[pl.ds(ks, tk), head_cols[hh]], qs[hh]) + bias
            scores.append(jnp.where(valid, s, MASK_VALUE))
        ms = tuple(update(hh, scores[hh], (q_start + d * tk) // tv, ms[hh]) for hh in range(hb))

    lam = _diff_lambda(lq_ref, lam0)
    for hh in range(hb):
        acc = acc_ref[hh]
        num = acc[0:B_HEAD_WIDTH]
        l = acc[B_HEAD_WIDTH:B_HEAD_WIDTH + 1]
        ot = num[:, :tq] / l[:, :tq] - lam * (num[:, tq:] / l[:, tq:])
        ot = ot * lax.rsqrt(jnp.mean(ot * ot, axis=0, keepdims=True) + NORM_EPS)
        o_ref[:, head_cols[hh]] = (ot.T * subg_ref[...] * (1.0 - lam0)).astype(BF16)


def _diff_prompt(qb, kb, vbt, lq, subg, lam0, layer, hb, tq, tk):
    b, t, _ = qb.shape
    nblk, tv = vbt.shape[1], vbt.shape[3]
    assert tk % tv == 0 and tq % tk == 0 and B_HEADS % hb == 0
    w = hb * B_HEAD_WIDTH
    blk = pl.BlockSpec((None, tq, w), lambda i, h, q: (i, q, h))
    return pl.pallas_call(
        functools.partial(_diff_prompt_kernel, lam0, hb, tq, tk),
        grid=(b, B_HEADS // hb, t // tq),
        in_specs=[pl.BlockSpec(memory_space=pltpu.SMEM), _layer(lq.shape[1:], layer),
                  _layer((1, B_HEAD_WIDTH), layer), blk,
                  pl.BlockSpec((None, t, w), lambda i, h, q: (i, 0, h)),
                  pl.BlockSpec((None, nblk, w, tv), lambda i, h, q: (i, 0, h, 0))],
        out_specs=blk,
        out_shape=jax.ShapeDtypeStruct((b, t, B_WIDTH), BF16),
        scratch_shapes=[pltpu.VMEM((hb, B_HEAD_WIDTH + SUM_ROWS, 2 * tq), F32)],
        compiler_params=_params("arbitrary", "arbitrary", "arbitrary"),
        name="diff_prompt",
    )(_alibi_slopes(B_HEADS), lq, subg.reshape(subg.shape[0], 1, B_HEAD_WIDTH), qb, kb, vbt)


def _diff_sample_kernel(lam0, nkc, slopes_ref, lq_ref, subg_ref, q_ref, kct_ref, vch_ref, kn_ref,
                        vn_ref, o_ref, m_ref, l_ref, acc_ref):
    kc = pl.program_id(1)
    t = q_ref.shape[0]
    tc = kct_ref.shape[1]
    past = tc * nkc

    @pl.when(kc == 0)
    def _():
        m_ref[...] = jnp.full(m_ref.shape, MASK_VALUE, F32)
        l_ref[...] = jnp.zeros(l_ref.shape, F32)
        acc_ref[...] = jnp.zeros(acc_ref.shape, F32)

    def update(h, s, v):
        width = s.shape[1]
        m_old = m_ref[h]
        m_new = jnp.maximum(m_old, jnp.max(s, axis=1, keepdims=True))
        alpha = jnp.exp2(m_old - m_new)
        if width % LANES == 0:
            p = jnp.exp2(s - jnp.tile(m_new, (1, width // LANES)))
        else:
            p = jnp.exp2(s - m_new[:, :width])
        v_and_ones = jnp.concatenate([v, jnp.ones(v.shape, BF16)], axis=1)
        pv = _dot(p.astype(BF16), v_and_ones)
        l_ref[h] = alpha * l_ref[h] + pv[:, B_HEAD_WIDTH:]
        acc_ref[h] = alpha * acc_ref[h] + pv[:, :B_HEAD_WIDTH]
        m_ref[h] = m_new

    kpos = (kc * tc - past + lax.broadcasted_iota(jnp.int32, (1, tc), 1)).astype(F32)
    scores = []
    for h in range(B_HEADS):
        cols = slice(h * B_HEAD_WIDTH, (h + 1) * B_HEAD_WIDTH)
        qs = _stack_components(q_ref[:, cols])
        scores.append(_dot(qs, kct_ref[cols, :].astype(BF16)) + (slopes_ref[h] * LOG2E) * kpos)
    for h in range(B_HEADS):
        update(h, scores[h], vch_ref[pl.ds(h, tc, stride=B_HEADS), :].astype(BF16))

    @pl.when(kc == nkc - 1)
    def _():
        lam = _diff_lambda(lq_ref, lam0)
        qpos = lax.broadcasted_iota(jnp.int32, (2 * t, t), 0) % t
        kidx = lax.broadcasted_iota(jnp.int32, (2 * t, t), 1)
        for h in range(B_HEADS):
            cols = slice(h * B_HEAD_WIDTH, (h + 1) * B_HEAD_WIDTH)
            qs = _stack_components(q_ref[:, cols])
            bias, _ = _own_block_bias_and_mask(slopes_ref[h] * LOG2E, qpos, kidx)
            update(h, _dot_nt(qs, kn_ref[:, cols]) + bias, vn_ref[:, cols])
            acc = acc_ref[h]
            l = l_ref[h]
            o = acc[:t] / l[:t] - lam * (acc[t:] / l[t:])
            o_ref[:, cols] = (_rms(o, subg_ref[...]) * (1.0 - lam0)).astype(BF16)


def _diff_sample(qb, kb, vb, cache_kt, cache_vh, lq, subg, lam0, layer, tc):
    b, t, _ = qb.shape
    past = cache_kt.shape[3]
    new = pl.BlockSpec((None, t, B_WIDTH), lambda i, k: (i, 0, 0))
    stat = pltpu.VMEM((B_HEADS, 2 * t, LANES), F32)
    return pl.pallas_call(
        functools.partial(_diff_sample_kernel, lam0, past // tc),
        grid=(b, past // tc),
        in_specs=[pl.BlockSpec(memory_space=pltpu.SMEM), _layer(lq.shape[1:], layer),
                  _layer((1, B_HEAD_WIDTH), layer), new,
                  pl.BlockSpec((None, None, B_WIDTH, tc), lambda i, k: (layer, i, 0, k)),
                  pl.BlockSpec((None, None, tc * B_HEADS, B_HEAD_WIDTH), lambda i, k: (layer, i, k, 0)),
                  new, new],
        out_specs=new,
        out_shape=jax.ShapeDtypeStruct((b, t, B_WIDTH), BF16),
        scratch_shapes=[stat, stat, pltpu.VMEM((B_HEADS, 2 * t, B_HEAD_WIDTH), F32)],
        compiler_params=_params("arbitrary", "arbitrary"),
        name="diff_sample",
    )(_alibi_slopes(B_HEADS), lq, subg.reshape(subg.shape[0], 1, B_HEAD_WIDTH), qb, cache_kt, cache_vh,
      kb, vb)


def _merge_kernel(ff_chunk, h_ref, gate_ref, oa_ref, ob_ref, ple_ref,
                  wba_ref, wbb_ref, wout_ref, wup_ref, wdown_ref, wple_ref, wpg_ref,
                  g_post_ref, g_fpre_ref, g_fpost_ref, g_ple_ref, o_ref):
    d = h_ref.shape[1]
    gate = gate_ref[...]
    mix = (gate[:, :d].astype(F32) * _dot(oa_ref[...], wba_ref[...])
           + gate[:, d:].astype(F32) * _dot(ob_ref[...], wbb_ref[...]))
    h = h_ref[...] + _rms(_dot(mix.astype(BF16), wout_ref[...]), g_post_ref[...])

    u = _rms(h, g_fpre_ref[...]).astype(BF16)
    f = jnp.zeros(h.shape, F32)
    for lo in range(0, wup_ref.shape[1], ff_chunk):
        act = jnp.square(jnp.maximum(_dot(u, wup_ref[:, lo:lo + ff_chunk]), 0.0))
        f = f + _dot(act.astype(BF16), wdown_ref[lo:lo + ff_chunk, :])
    h = h + _rms(f, g_fpost_ref[...])

    pg = jax.nn.sigmoid(_dot(h.astype(BF16), wpg_ref[...]))
    pe = _rms(_dot(ple_ref[...].astype(BF16), wple_ref[...]), g_ple_ref[...])
    o_ref[...] = h + pg * pe


def _merge(h, gate, oa, ob, ple, w, gains, layer, tm, ff_chunk):
    n, d = h.shape
    row = lambda a: pl.BlockSpec((tm, a.shape[1]), lambda i: (i, 0))
    acts = [h, gate, oa, ob]
    gains = [g.reshape(g.shape[0], 1, d) for g in gains]
    return pl.pallas_call(
        functools.partial(_merge_kernel, ff_chunk),
        grid=(n // tm,),
        in_specs=([row(a) for a in acts]
                  + [pl.BlockSpec((None, tm, ple.shape[2]), lambda i: (layer, i, 0))]
                  + [_layer(x.shape[1:], layer) for x in w] + [_layer((1, d), layer)] * 4),
        out_specs=pl.BlockSpec((tm, d), lambda i: (i, 0)),
        out_shape=jax.ShapeDtypeStruct((n, d), F32),
        compiler_params=_params("arbitrary"),
        name="merge",
    )(*acts, ple, *w, *gains)


ROW_TILE = 512
MERGE_TILE = 512
FF_CHUNK = 1024
BAND_CHUNKS = 4
DIFF_HEADS = 4
DIFF_TQ = 512
DIFF_TK = 512
DIFF_SAMPLE_TC = 2048


def kernel(x_prompt, x_sample, cache_a_k, cache_a_v, cache_b_k, cache_b_v, p_prompt, p_sample,
           w_in, rel_bias, diff_lambda, diff_subln, w_gate, w_branch_a, w_branch_b, w_out,
           g_mix_pre, g_mix_post, g_ffn_pre, g_ffn_post, w_up, w_down, w_ple, g_ple, w_ple_gate):
    depth = w_in.shape[0]
    bp, tp, d = x_prompt.shape
    bs, ts, _ = x_sample.shape
    past = cache_b_k.shape[2]
    a_win = cache_a_k.shape[2]
    keep = min(A_WIN, tp)
    assert a_win == A_WIN and keep == A_WIN and ts == CHUNK and tp % DIFF_TQ == 0

    w_in_b, w_gate_b = w_in.astype(BF16), w_gate.astype(BF16)
    merge_w = tuple(x.astype(BF16) for x in (w_branch_a, w_branch_b, w_out, w_up, w_down, w_ple,
                                             w_ple_gate))
    gains = (g_mix_post, g_ffn_pre, g_ffn_post, g_ple)

    cache_a_kt = jnp.transpose(cache_a_k, (0, 1, 3, 4, 2)).reshape(depth, bs, A_WIDTH, a_win)
    cache_a_vt = jnp.transpose(cache_a_v, (0, 1, 3, 4, 2)).reshape(depth, bs, A_WIDTH, a_win)
    cache_b_kt = jnp.transpose(cache_b_k, (0, 1, 3, 4, 5, 2)).reshape(depth, bs, B_WIDTH, past)
    cache_b_vh = cache_b_v.reshape(depth, bs, past * B_HEADS, B_HEAD_WIDTH)
    ple_p = p_prompt.reshape(depth, bp * tp, -1)
    ple_s = p_sample.reshape(depth, bs * ts, -1)

    hp = x_prompt.reshape(bp * tp, d)
    hs = x_sample.reshape(bs * ts, d)
    tails = ([], [])
    stacked_p = stacked_s = None

    for i in range(depth):
        lam0 = _lambda_init(i)
        bias = _band_bias(rel_bias[i])

        akt, avt = _proj_tail(hp.reshape(bp, tp, d), g_mix_pre, w_in_b, i, keep)
        tails[0].append(akt)
        tails[1].append(avt)
        (qa, kab, vab, qb, kbb, vbt, gate), stacked_p = _proj(hp, g_mix_pre, w_in_b, w_gate_b, i,
                                                              ROW_TILE, tp, stacked_p)
        r3 = lambda a: a.reshape(bp, tp, a.shape[-1])
        oa = _band_prompt(r3(qa), r3(kab), r3(vab), bias, BAND_CHUNKS)
        ob = _diff_prompt(r3(qb), r3(kbb), vbt, diff_lambda, diff_subln, lam0, i, DIFF_HEADS, DIFF_TQ,
                          DIFF_TK)
        hp = _merge(hp, gate, oa.reshape(bp * tp, -1), ob.reshape(bp * tp, -1), ple_p, merge_w, gains,
                    i, MERGE_TILE, FF_CHUNK)

        (qa, kab, vab, qb, kbb, vbb, gate), stacked_s = _proj(hs, g_mix_pre, w_in_b, w_gate_b, i,
                                                              ROW_TILE, None, stacked_s)
        r3 = lambda a: a.reshape(bs, ts, a.shape[-1])
        oa = _band_sample(r3(qa), r3(kab), r3(vab), cache_a_kt, cache_a_vt, bias, i)
        ob = _diff_sample(r3(qb), r3(kbb), r3(vbb), cache_b_kt, cache_b_vh, diff_lambda, diff_subln,
                          lam0, i, DIFF_SAMPLE_TC)
        hs = _merge(hs, gate, oa.reshape(bs * ts, -1), ob.reshape(bs * ts, -1), ple_s, merge_w, gains,
                    i, MERGE_TILE, FF_CHUNK)

    kbt_p, vbh_p = stacked_p
    kaf_s, vaf_s, kbf_s, vbh_s = stacked_s
    a_feature_major = lambda a: jnp.transpose(
        jnp.stack(a).reshape(depth, bp, A_HEADS, HEAD_DIM, keep), (0, 1, 4, 2, 3))
    return (hp.reshape(bp, tp, d), hs.reshape(bs, ts, d),
            a_feature_major(tails[0]), a_feature_major(tails[1]),
            jnp.transpose(kbt_p.reshape(depth, bp, B_HEADS, 2, HEAD_DIM, tp), (0, 1, 5, 2, 3, 4)),
            vbh_p.reshape(depth, bp, tp, B_HEADS, B_HEAD_WIDTH),
            kaf_s.reshape(depth, bs, ts, A_HEADS, HEAD_DIM), vaf_s.reshape(depth, bs, ts, A_HEADS, HEAD_DIM),
            kbf_s.reshape(depth, bs, ts, B_HEADS, 2, HEAD_DIM),
            vbh_s.reshape(depth, bs, ts, B_HEADS, B_HEAD_WIDTH))
```

```python
import functools
import math

import jax
import jax.numpy as jnp
import numpy as np
from jax import lax
from jax.experimental import pallas as pl
from jax.experimental.pallas import tpu as pltpu

CHUNK = 64
HEAD_DIM = 64
A_HEADS = 8
A_BAND = 8
A_WIDTH = A_HEADS * HEAD_DIM
A_WIN = A_BAND * CHUNK
A_KEYS = A_WIN + CHUNK
REL_CLIP = 128
B_HEADS = 8
B_HEAD_WIDTH = 2 * HEAD_DIM
B_WIDTH = B_HEADS * B_HEAD_WIDTH
NORM_EPS = 1e-6
MASK_VALUE = -1e30
QK_SCALE = HEAD_DIM ** -0.5
LOG2E = math.log2(math.e)
STAGE_SKEW_SCORES = 1
STAGE_SKEW_PV = 0
BAND_SCORE_SKEW = 1
SAMPLE_SCORE_SKEW = 8
SUM_ROWS = 16

A_GROUP = 4
A_GROUP_WIDTH = A_GROUP * HEAD_DIM
LANES = 128

V7X_VMEM_LIMIT_BYTES = 60000 * 1024

BF16 = jnp.bfloat16
F32 = jnp.float32


def _lambda_init(layer):
    return 0.8 - 0.6 * math.exp(-0.3 * layer)


def _alibi_slopes(n):
    return jnp.asarray(2.0 ** (-8.0 * np.arange(1, n + 1) / n), dtype=F32)


def _rms(x, g):
    return x * lax.rsqrt(jnp.mean(x * x, axis=-1, keepdims=True) + NORM_EPS) * g


def _dot(a, b):
    return jnp.dot(a, b, preferred_element_type=F32)


def _dot_nt(a, b):
    return lax.dot_general(a, b, (((1,), (1,)), ((), ())), preferred_element_type=F32)


def _resident(shape):
    nd = len(shape)
    return pl.BlockSpec(shape, lambda *_: (0,) * nd, pipeline_mode=pl.Buffered(1))


def _layer(shape, layer):
    nd = len(shape)
    return pl.BlockSpec((None,) + tuple(shape), lambda *_: (layer,) + (0,) * nd,
                        pipeline_mode=pl.Buffered(1))


def _params(*sem):
    return pltpu.CompilerParams(dimension_semantics=sem, vmem_limit_bytes=V7X_VMEM_LIMIT_BYTES)


IN_QA, IN_KA, IN_VA = 0, A_WIDTH, 2 * A_WIDTH
IN_QB, IN_KB, IN_VB = 3 * A_WIDTH, 3 * A_WIDTH + B_WIDTH, 3 * A_WIDTH + 2 * B_WIDTH


def _store_head_tiled(o_ref, x):
    rows = x.shape[0]
    for h in range(B_HEADS):
        o_ref[pl.ds(h, rows, stride=B_HEADS), :] = x[:, h * B_HEAD_WIDTH:(h + 1) * B_HEAD_WIDTH]


def _proj_kernel(prompt, n_carried, layer, h_ref, g_ref, win_ref, wg_ref, *refs):
    out_refs = refs[n_carried:]

    def stacked(ref):
        if n_carried:
            return ref
        for other in range(ref.shape[0]):
            if other != layer:
                ref[other] = jnp.zeros(ref.shape[1:], ref.dtype)
        return ref.at[layer]

    u = _rms(h_ref[...], g_ref[...]).astype(BF16)

    def col(lo, width):
        return _dot(u, win_ref[:, lo:lo + width])

    qa = (col(IN_QA, A_WIDTH) * QK_SCALE).astype(BF16)
    ka = col(IN_KA, A_WIDTH)
    va = col(IN_VA, A_WIDTH)
    qb = (col(IN_QB, B_WIDTH) * (QK_SCALE * LOG2E)).astype(BF16)
    kb = col(IN_KB, B_WIDTH)
    vb = col(IN_VB, B_WIDTH)
    gate = jax.nn.sigmoid(_dot(u, wg_ref[...])).astype(BF16)
    if prompt:
        qa_ref, kab_ref, vab_ref, qb_ref, kbb_ref, kbt_ref, vbt_ref, vbh_ref, gate_ref = out_refs
        stacked(kbt_ref)[...] = kb.T
        vbt_ref[...] = vb.T.astype(BF16)
    else:
        (qa_ref, kab_ref, vab_ref, qb_ref, kbb_ref, kaf_ref, vaf_ref, kbf_ref, vbb_ref, vbh_ref,
         gate_ref) = out_refs
        stacked(kaf_ref)[...] = ka
        stacked(vaf_ref)[...] = va
        stacked(kbf_ref)[...] = kb
        vbb_ref[...] = vb.astype(BF16)
    qa_ref[...] = qa
    kab_ref[...] = ka.astype(BF16)
    vab_ref[...] = va.astype(BF16)
    qb_ref[...] = qb
    kbb_ref[...] = kb.astype(BF16)
    _store_head_tiled(stacked(vbh_ref), vb)
    gate_ref[...] = gate


def _proj(h, g, w_in, w_gate, layer, tm, prompt_seq, carried):
    n, d = h.shape
    depth, _, gate_w = w_gate.shape
    first = carried is None
    row = lambda w: pl.BlockSpec((tm, w), lambda i: (i, 0))
    sds = jax.ShapeDtypeStruct
    tok = lambda w, dt: (row(w), sds((n, w), dt))

    def layer_block(block, index_map):
        if first:
            return pl.BlockSpec((depth,) + block, lambda i: (0,) + index_map(i))
        return pl.BlockSpec((None,) + block, lambda i: (layer,) + index_map(i))

    tok_stacked = lambda w: (layer_block((tm, w), lambda i: (i, 0)), sds((depth, n, w), F32))
    head_tiled = (layer_block((tm * B_HEADS, B_HEAD_WIDTH), lambda i: (i, 0)),
                  sds((depth, n * B_HEADS, B_HEAD_WIDTH), F32))
    local = [tok(A_WIDTH, BF16), tok(A_WIDTH, BF16), tok(A_WIDTH, BF16), tok(B_WIDTH, BF16),
             tok(B_WIDTH, BF16)]
    if prompt_seq is not None:
        nblk = prompt_seq // tm
        b = n // prompt_seq
        outs = local + [
            (layer_block((None, B_WIDTH, tm), lambda i: (i // nblk, 0, i % nblk)),
             sds((depth, b, B_WIDTH, prompt_seq), F32)),
            (pl.BlockSpec((None, None, B_WIDTH, tm), lambda i: (i // nblk, i % nblk, 0, 0)),
             sds((b, nblk, B_WIDTH, tm), BF16)),
            head_tiled, tok(gate_w, BF16)]
        stacked_idx = [5, 7]
    else:
        outs = local + [tok_stacked(A_WIDTH), tok_stacked(A_WIDTH), tok_stacked(B_WIDTH),
                        tok(B_WIDTH, BF16), head_tiled, tok(gate_w, BF16)]
        stacked_idx = [5, 6, 7, 9]
    n_fixed_inputs = 4
    carried = [] if carried is None else list(carried)
    aliases = {n_fixed_inputs + k: stacked_idx[k] for k in range(len(carried))}
    res = pl.pallas_call(
        functools.partial(_proj_kernel, prompt_seq is not None, len(carried), layer),
        grid=(n // tm,),
        in_specs=[row(d), _layer((1, d), layer), _layer(w_in.shape[1:], layer),
                  _layer(w_gate.shape[1:], layer)] + [pl.BlockSpec(memory_space=pl.ANY)] * len(carried),
        out_specs=[s for s, _ in outs],
        out_shape=[s for _, s in outs],
        input_output_aliases=aliases,
        compiler_params=_params("arbitrary"),
        name="proj",
    )(h, g.reshape(g.shape[0], 1, d), w_in, w_gate, *carried)
    return ([r for k, r in enumerate(res) if k not in stacked_idx], [res[k] for k in stacked_idx])


def _proj_tail_kernel(h_ref, g_ref, wk_ref, wv_ref, kt_ref, vt_ref):
    u = _rms(h_ref[...], g_ref[...]).astype(BF16)
    kt_ref[...] = _dot(u, wk_ref[...]).T
    vt_ref[...] = _dot(u, wv_ref[...]).T


def _proj_tail(h3, g, w_in, layer, keep):
    b, t, d = h3.shape
    last = t // keep - 1
    wcol = lambda c: pl.BlockSpec((None, d, A_WIDTH), lambda i: (layer, 0, c), pipeline_mode=pl.Buffered(1))
    out = pl.BlockSpec((None, A_WIDTH, keep), lambda i: (i, 0, 0))
    return pl.pallas_call(
        _proj_tail_kernel,
        grid=(b,),
        in_specs=[pl.BlockSpec((None, keep, d), lambda i: (i, last, 0)), _layer((1, d), layer),
                  wcol(IN_KA // A_WIDTH), wcol(IN_VA // A_WIDTH)],
        out_specs=[out, out],
        out_shape=[jax.ShapeDtypeStruct((b, A_WIDTH, keep), F32)] * 2,
        compiler_params=_params("arbitrary"),
        name="proj_tail",
    )(h3, g.reshape(g.shape[0], 1, d), w_in, w_in)


def _band_bias_kernel(tab_ref, o_ref):
    qi = lax.broadcasted_iota(jnp.int32, (CHUNK, A_KEYS), 0)
    ki = lax.broadcasted_iota(jnp.int32, (CHUNK, A_KEYS), 1)
    idx = jnp.clip(A_WIN + qi - ki, -REL_CLIP, REL_CLIP) + REL_CLIP
    lo = REL_CLIP - CHUNK + 1
    hi = 2 * REL_CLIP

    for h in range(A_HEADS):
        g, hh = divmod(h, A_GROUP)
        o_ref[g, hh * CHUNK:(hh + 1) * CHUNK, :] = jnp.full((CHUNK, A_KEYS), tab_ref[h, hi], F32)

    def body(r, carry):
        hit = idx == r
        for h in range(A_HEADS):
            g, hh = divmod(h, A_GROUP)
            sl = (g, slice(hh * CHUNK, (hh + 1) * CHUNK), slice(None))
            o_ref[sl] = jnp.where(hit, tab_ref[h, r], o_ref[sl])
        return carry

    lax.fori_loop(lo, hi, body, 0)


def _band_bias(table):
    return pl.pallas_call(
        _band_bias_kernel,
        in_specs=[pl.BlockSpec(memory_space=pltpu.SMEM)],
        out_specs=pl.BlockSpec(memory_space=pltpu.VMEM),
        out_shape=jax.ShapeDtypeStruct((A_HEADS // A_GROUP, A_GROUP * CHUNK, A_KEYS), F32),
        name="band_bias",
    )(table)


def _block_diag_query(qg):
    lane_head = lax.broadcasted_iota(jnp.int32, qg.shape, 1) // HEAD_DIM
    return jnp.concatenate(
        [jnp.where(lane_head == hh, qg, jnp.zeros_like(qg)) for hh in range(A_GROUP)], axis=0)


def _gather_heads(o):
    lane_head = lax.broadcasted_iota(jnp.int32, (CHUNK, A_GROUP_WIDTH), 1) // HEAD_DIM
    og = jnp.zeros((CHUNK, A_GROUP_WIDTH), F32)
    for hh in range(A_GROUP):
        og = og + jnp.where(lane_head == hh, o[hh * CHUNK:(hh + 1) * CHUNK, :], 0.0)
    return og


def _band_prompt_kernel(nc, q_ref, k_ref, v_ref, bias_ref, o_ref, kpad_ref, vpad_ref):
    step = pl.program_id(1)
    t = k_ref.shape[0]

    @pl.when(step == 0)
    def _():
        kpad_ref[0:A_WIN, :] = jnp.zeros((A_WIN, A_WIDTH), BF16)
        vpad_ref[0:A_WIN, :] = jnp.zeros((A_WIN, A_WIDTH), BF16)
        kpad_ref[A_WIN:A_WIN + t, :] = k_ref[...]
        vpad_ref[A_WIN:A_WIN + t, :] = v_ref[...]

    chains = [(e, g) for e in range(nc) for g in range(A_HEADS // A_GROUP)]

    def window_start(e):
        return pl.multiple_of((step * nc + e) * CHUNK, CHUNK)

    def score(e, g):
        cols = slice(g * A_GROUP_WIDTH, (g + 1) * A_GROUP_WIDTH)
        q = q_ref[e * CHUNK:(e + 1) * CHUNK, cols]
        kw = kpad_ref[pl.ds(window_start(e), A_KEYS), cols]
        s = _dot_nt(_block_diag_query(q), kw) + bias_ref[g]
        kidx = lax.broadcasted_iota(jnp.int32, s.shape, 1)
        return jnp.where(kidx >= (A_BAND - (step * nc + e)) * CHUNK, s, MASK_VALUE)

    def finish(e, g, s):
        cols = slice(g * A_GROUP_WIDTH, (g + 1) * A_GROUP_WIDTH)
        m = jnp.max(s, axis=-1, keepdims=True)
        p = jnp.exp(s - m)
        l = jnp.sum(p, axis=-1, keepdims=True)
        vw = vpad_ref[pl.ds(window_start(e), A_KEYS), cols]
        o = _gather_heads(_dot(p.astype(BF16), vw) / l)
        o_ref[e * CHUNK:(e + 1) * CHUNK, cols] = o.astype(BF16)

    scores = {i: score(*chains[i]) for i in range(min(BAND_SCORE_SKEW, len(chains)))}
    for i, (e, g) in enumerate(chains):
        if i + BAND_SCORE_SKEW < len(chains):
            scores[i + BAND_SCORE_SKEW] = score(*chains[i + BAND_SCORE_SKEW])
        finish(e, g, scores.pop(i))


def _band_prompt(qa, ka, va, bias, nc):
    b, t, _ = qa.shape
    rows = nc * CHUNK
    return pl.pallas_call(
        functools.partial(_band_prompt_kernel, nc),
        grid=(b, t // rows),
        in_specs=[pl.BlockSpec((None, rows, A_WIDTH), lambda i, c: (i, c, 0)),
                  pl.BlockSpec((None, t, A_WIDTH), lambda i, c: (i, 0, 0)),
                  pl.BlockSpec((None, t, A_WIDTH), lambda i, c: (i, 0, 0)),
                  _resident(bias.shape)],
        out_specs=pl.BlockSpec((None, rows, A_WIDTH), lambda i, c: (i, c, 0)),
        out_shape=jax.ShapeDtypeStruct((b, t, A_WIDTH), BF16),
        scratch_shapes=[pltpu.VMEM((A_WIN + t, A_WIDTH), BF16),
                        pltpu.VMEM((A_WIN + t, A_WIDTH), BF16)],
        compiler_params=_params("arbitrary", "arbitrary"),
        name="band_prompt",
    )(qa, ka, va, bias)


def _band_sample_kernel(q_ref, kct_ref, vct_ref, kn_ref, vn_ref, bias_ref, o_ref):
    q = q_ref[...]
    outs = []
    for g in range(A_HEADS // A_GROUP):
        cols = slice(g * A_GROUP_WIDTH, (g + 1) * A_GROUP_WIDTH)
        qbd = _block_diag_query(q[:, cols])
        s_old = _dot(qbd, kct_ref[cols, :].astype(BF16)) + bias_ref[g, :, 0:A_WIN]
        s_new = _dot_nt(qbd, kn_ref[:, cols]) + bias_ref[g, :, A_WIN:A_KEYS]
        m = jnp.maximum(jnp.max(s_old, axis=-1, keepdims=True), jnp.max(s_new, axis=-1, keepdims=True))
        p_old = jnp.exp(s_old - m)
        p_new = jnp.exp(s_new - m)
        l = jnp.sum(p_old, axis=-1, keepdims=True) + jnp.sum(p_new, axis=-1, keepdims=True)
        o = (_dot_nt(p_old.astype(BF16), vct_ref[cols, :].astype(BF16))
             + _dot(p_new.astype(BF16), vn_ref[:, cols]))
        outs.append(_gather_heads(o / l))
    o_ref[...] = jnp.concatenate(outs, axis=1).astype(BF16)


def _band_sample(qa, ka, va, cache_kt, cache_vt, bias, layer):
    b, t, _ = qa.shape
    w = cache_kt.shape[3]
    new = pl.BlockSpec((None, t, A_WIDTH), lambda i: (i, 0, 0))
    old = pl.BlockSpec((None, None, A_WIDTH, w), lambda i: (layer, i, 0, 0))
    return pl.pallas_call(
        _band_sample_kernel,
        grid=(b,),
        in_specs=[new, old, old, new, new, _resident(bias.shape)],
        out_specs=new,
        out_shape=jax.ShapeDtypeStruct((b, t, A_WIDTH), BF16),
        compiler_params=_params("arbitrary"),
        name="band_sample",
    )(qa, cache_kt, cache_vt, ka, va, bias)


def _diff_lambda(lq_ref, lam0):
    lq = lq_ref[...]
    a = jnp.sum(lq[0:1, :] * lq[1:2, :], axis=-1, keepdims=True)
    b = jnp.sum(lq[2:3, :] * lq[3:4, :], axis=-1, keepdims=True)
    return jnp.exp(a) - jnp.exp(b) + lam0


def _stack_components(q):
    lane = lax.broadcasted_iota(jnp.int32, q.shape, 1)
    zero = jnp.zeros_like(q)
    return jnp.concatenate([jnp.where(lane < HEAD_DIM, q, zero),
                            jnp.where(lane >= HEAD_DIM, q, zero)], axis=0)


def _own_block_bias_and_mask(slope, qpos, kpos):
    bias = slope * jnp.minimum(kpos, 2 * qpos - kpos).astype(F32)
    valid = (kpos // CHUNK) <= (qpos // CHUNK)
    return bias, valid


def _diff_prompt_kernel(lam0, hb, tq, tk, slopes_ref, lq_ref, subg_ref, q_ref, k_ref, vt_ref, o_ref,
                        acc_ref):
    hg = pl.program_id(1)
    qi = pl.program_id(2)
    q_start = qi * tq
    tv = vt_ref.shape[2]
    lane_reps = 2 * tq // LANES
    key_iota = lax.broadcasted_iota(jnp.int32, (tk, LANES), 0)
    ones_rows = jnp.ones((SUM_ROWS, tv), BF16)
    head_cols = [slice(hh * B_HEAD_WIDTH, (hh + 1) * B_HEAD_WIDTH) for hh in range(hb)]
    slopes = [slopes_ref[hg * hb + hh] * LOG2E for hh in range(hb)]
    qs = [_stack_components(q_ref[:, c]) for c in head_cols]

    acc_ref[...] = jnp.zeros(acc_ref.shape, F32)

    def softmax_stage(s, m_old):
        m_new = jnp.maximum(m_old, jnp.max(s, axis=0, keepdims=True))
        return m_new, jnp.exp2(m_old - m_new), jnp.exp2(s - m_new).astype(BF16)

    def pv_stage(hh, first_v_block, alpha, pb):
        pv = None
        for u in range(tk // tv):
            vt = jnp.concatenate([vt_ref[first_v_block + u, head_cols[hh], :], ones_rows], axis=0)
            part = _dot(vt, pb[u * tv:(u + 1) * tv])
            pv = part if pv is None else pv + part
        acc_ref[hh] = alpha * acc_ref[hh] + pv

    def run_block(score, first_v_block, ms):
        scores, probs, out = {}, {}, []
        for step in range(hb + STAGE_SKEW_SCORES + STAGE_SKEW_PV):
            hs, he, hp = step, step - STAGE_SKEW_SCORES, step - STAGE_SKEW_SCORES - STAGE_SKEW_PV
            if hs < hb:
                scores[hs] = score(hs)
            if 0 <= he < hb:
                m_new, alpha, pb = softmax_stage(scores.pop(he), ms[he])
                out.append(m_new)
                probs[he] = (alpha, pb)
            if 0 <= hp < hb:
                pv_stage(hp, first_v_block, *probs.pop(hp))
        return tuple(out)

    def below(j, ms):
        ks = pl.multiple_of(j * tk, tk)

        def score(hh):
            kbias = slopes[hh] * (key_iota + (ks - q_start)).astype(F32)
            return (_dot_nt(k_ref[pl.ds(ks, tk), head_cols[hh]], qs[hh])
                    + jnp.tile(kbias, (1, lane_reps)))

        return run_block(score, j * (tk // tv), ms)

    m0 = jnp.full((1, 2 * tq), MASK_VALUE, F32)
    ms = lax.fori_loop(0, q_start // tk, below, (m0,) * hb)

    for d in range(tq // tk):
        ks = pl.multiple_of(q_start + d * tk, tk)
        kpos = lax.broadcasted_iota(jnp.int32, (tk, tq), 0) + d * tk
        qpos = lax.broadcasted_iota(jnp.int32, (tk, tq), 1)

        def score(hh):
            bias, valid = _own_block_bias_and_mask(slopes[hh], qpos, kpos)
            bias = jnp.concatenate([bias, bias], axis=1)
            valid = jnp.concatenate([valid, valid], axis=1)
            s = _dot_nt(k_ref[pl.ds(ks, tk), head_cols[hh]], qs[hh]) + bias
            return jnp.where(valid, s, MASK_VALUE)

        ms = run_block(score, (q_start + d * tk) // tv, ms)

    lam = _diff_lambda(lq_ref, lam0)
    for hh in range(hb):
        acc = acc_ref[hh]
        num = acc[0:B_HEAD_WIDTH]
        l = acc[B_HEAD_WIDTH:B_HEAD_WIDTH + 1]
        ot = num[:, :tq] / l[:, :tq] - lam * (num[:, tq:] / l[:, tq:])
        ot = ot * lax.rsqrt(jnp.mean(ot * ot, axis=0, keepdims=True) + NORM_EPS)
        o_ref[:, head_cols[hh]] = (ot.T * subg_ref[...] * (1.0 - lam0)).astype(BF16)


def _diff_prompt(qb, kb, vbt, lq, subg, lam0, layer, hb, tq, tk):
    b, t, _ = qb.shape
    nblk, tv = vbt.shape[1], vbt.shape[3]
    assert tk % tv == 0 and tq % tk == 0 and B_HEADS % hb == 0
    w = hb * B_HEAD_WIDTH
    blk = pl.BlockSpec((None, tq, w), lambda i, h, q: (i, q, h))
    return pl.pallas_call(
        functools.partial(_diff_prompt_kernel, lam0, hb, tq, tk),
        grid=(b, B_HEADS // hb, t // tq),
        in_specs=[pl.BlockSpec(memory_space=pltpu.SMEM), _layer(lq.shape[1:], layer),
                  _layer((1, B_HEAD_WIDTH), layer), blk,
                  pl.BlockSpec((None, t, w), lambda i, h, q: (i, 0, h)),
                  pl.BlockSpec((None, nblk, w, tv), lambda i, h, q: (i, 0, h, 0))],
        out_specs=blk,
        out_shape=jax.ShapeDtypeStruct((b, t, B_WIDTH), BF16),
        scratch_shapes=[pltpu.VMEM((hb, B_HEAD_WIDTH + SUM_ROWS, 2 * tq), F32)],
        compiler_params=_params("arbitrary", "arbitrary", "arbitrary"),
        name="diff_prompt",
    )(_alibi_slopes(B_HEADS), lq, subg.reshape(subg.shape[0], 1, B_HEAD_WIDTH), qb, kb, vbt)


def _diff_sample_kernel(lam0, nkc, slopes_ref, lq_ref, subg_ref, q_ref, kct_ref, vch_ref, kn_ref,
                        vn_ref, o_ref, m_ref, l_ref, acc_ref):
    kc = pl.program_id(1)
    t = q_ref.shape[0]
    tc = kct_ref.shape[1]
    past = tc * nkc

    @pl.when(kc == 0)
    def _():
        m_ref[...] = jnp.full(m_ref.shape, MASK_VALUE, F32)
        l_ref[...] = jnp.zeros(l_ref.shape, F32)
        acc_ref[...] = jnp.zeros(acc_ref.shape, F32)

    def update(h, s, v):
        width = s.shape[1]
        m_old = m_ref[h]
        m_new = jnp.maximum(m_old, jnp.max(s, axis=1, keepdims=True))
        alpha = jnp.exp2(m_old - m_new)
        if width % LANES == 0:
            p = jnp.exp2(s - jnp.tile(m_new, (1, width // LANES)))
        else:
            p = jnp.exp2(s - m_new[:, :width])
        v_and_ones = jnp.concatenate([v, jnp.ones(v.shape, BF16)], axis=1)
        pv = _dot(p.astype(BF16), v_and_ones)
        l_ref[h] = alpha * l_ref[h] + pv[:, B_HEAD_WIDTH:]
        acc_ref[h] = alpha * acc_ref[h] + pv[:, :B_HEAD_WIDTH]
        m_ref[h] = m_new

    kpos = (kc * tc - past + lax.broadcasted_iota(jnp.int32, (1, tc), 1)).astype(F32)
    def score(h):
        cols = slice(h * B_HEAD_WIDTH, (h + 1) * B_HEAD_WIDTH)
        qs = _stack_components(q_ref[:, cols])
        return _dot(qs, kct_ref[cols, :].astype(BF16)) + (slopes_ref[h] * LOG2E) * kpos

    scores = {h: score(h) for h in range(min(SAMPLE_SCORE_SKEW, B_HEADS))}
    for h in range(B_HEADS):
        if h + SAMPLE_SCORE_SKEW < B_HEADS:
            scores[h + SAMPLE_SCORE_SKEW] = score(h + SAMPLE_SCORE_SKEW)
        update(h, scores.pop(h), vch_ref[pl.ds(h, tc, stride=B_HEADS), :].astype(BF16))

    @pl.when(kc == nkc - 1)
    def _():
        lam = _diff_lambda(lq_ref, lam0)
        qpos = lax.broadcasted_iota(jnp.int32, (2 * t, t), 0) % t
        kidx = lax.broadcasted_iota(jnp.int32, (2 * t, t), 1)
        for h in range(B_HEADS):
            cols = slice(h * B_HEAD_WIDTH, (h + 1) * B_HEAD_WIDTH)
            qs = _stack_components(q_ref[:, cols])
            bias, _ = _own_block_bias_and_mask(slopes_ref[h] * LOG2E, qpos, kidx)
            update(h, _dot_nt(qs, kn_ref[:, cols]) + bias, vn_ref[:, cols])
            acc = acc_ref[h]
            l = l_ref[h]
            o = acc[:t] / l[:t] - lam * (acc[t:] / l[t:])
            o_ref[:, cols] = (_rms(o, subg_ref[...]) * (1.0 - lam0)).astype(BF16)


def _diff_sample(qb, kb, vb, cache_kt, cache_vh, lq, subg, lam0, layer, tc):
    b, t, _ = qb.shape
    past = cache_kt.shape[3]
    new = pl.BlockSpec((None, t, B_WIDTH), lambda i, k: (i, 0, 0))
    stat = pltpu.VMEM((B_HEADS, 2 * t, LANES), F32)
    return pl.pallas_call(
        functools.partial(_diff_sample_kernel, lam0, past // tc),
        grid=(b, past // tc),
        in_specs=[pl.BlockSpec(memory_space=pltpu.SMEM), _layer(lq.shape[1:], layer),
                  _layer((1, B_HEAD_WIDTH), layer), new,
                  pl.BlockSpec((None, None, B_WIDTH, tc), lambda i, k: (layer, i, 0, k)),
                  pl.BlockSpec((None, None, tc * B_HEADS, B_HEAD_WIDTH), lambda i, k: (layer, i, k, 0)),
                  new, new],
        out_specs=new,
        out_shape=jax.ShapeDtypeStruct((b, t, B_WIDTH), BF16),
        scratch_shapes=[stat, stat, pltpu.VMEM((B_HEADS, 2 * t, B_HEAD_WIDTH), F32)],
        compiler_params=_params("arbitrary", "arbitrary"),
        name="diff_sample",
    )(_alibi_slopes(B_HEADS), lq, subg.reshape(subg.shape[0], 1, B_HEAD_WIDTH), qb, cache_kt, cache_vh,
      kb, vb)


def _merge_kernel(ff_chunk, h_ref, gate_ref, oa_ref, ob_ref, ple_ref,
                  wba_ref, wbb_ref, wout_ref, wup_ref, wdown_ref, wple_ref, wpg_ref,
                  g_post_ref, g_fpre_ref, g_fpost_ref, g_ple_ref, o_ref):
    d = h_ref.shape[1]
    gate = gate_ref[...]
    mix = (gate[:, :d].astype(F32) * _dot(oa_ref[...], wba_ref[...])
           + gate[:, d:].astype(F32) * _dot(ob_ref[...], wbb_ref[...]))
    h = h_ref[...] + _rms(_dot(mix.astype(BF16), wout_ref[...]), g_post_ref[...])

    u = _rms(h, g_fpre_ref[...]).astype(BF16)
    f = jnp.zeros(h.shape, F32)
    for lo in range(0, wup_ref.shape[1], ff_chunk):
        act = jnp.square(jnp.maximum(_dot(u, wup_ref[:, lo:lo + ff_chunk]), 0.0))
        f = f + _dot(act.astype(BF16), wdown_ref[lo:lo + ff_chunk, :])
    h = h + _rms(f, g_fpost_ref[...])

    pg = jax.nn.sigmoid(_dot(h.astype(BF16), wpg_ref[...]))
    pe = _rms(_dot(ple_ref[...].astype(BF16), wple_ref[...]), g_ple_ref[...])
    o_ref[...] = h + pg * pe


def _merge(h, gate, oa, ob, ple, w, gains, layer, tm, ff_chunk):
    n, d = h.shape
    row = lambda a: pl.BlockSpec((tm, a.shape[1]), lambda i: (i, 0))
    acts = [h, gate, oa, ob]
    gains = [g.reshape(g.shape[0], 1, d) for g in gains]
    return pl.pallas_call(
        functools.partial(_merge_kernel, ff_chunk),
        grid=(n // tm,),
        in_specs=([row(a) for a in acts]
                  + [pl.BlockSpec((None, tm, ple.shape[2]), lambda i: (layer, i, 0))]
                  + [_layer(x.shape[1:], layer) for x in w] + [_layer((1, d), layer)] * 4),
        out_specs=pl.BlockSpec((tm, d), lambda i: (i, 0)),
        out_shape=jax.ShapeDtypeStruct((n, d), F32),
        compiler_params=_params("arbitrary"),
        name="merge",
    )(*acts, ple, *w, *gains)


ROW_TILE = 512
MERGE_TILE = 512
FF_CHUNK = 1024
BAND_CHUNKS = 8
DIFF_HEADS = 4
DIFF_TQ = 512
DIFF_TK = 512
DIFF_SAMPLE_TC = 2048


def kernel(x_prompt, x_sample, cache_a_k, cache_a_v, cache_b_k, cache_b_v, p_prompt, p_sample,
           w_in, rel_bias, diff_lambda, diff_subln, w_gate, w_branch_a, w_branch_b, w_out,
           g_mix_pre, g_mix_post, g_ffn_pre, g_ffn_post, w_up, w_down, w_ple, g_ple, w_ple_gate):
    depth = w_in.shape[0]
    bp, tp, d = x_prompt.shape
    bs, ts, _ = x_sample.shape
    past = cache_b_k.shape[2]
    a_win = cache_a_k.shape[2]
    keep = min(A_WIN, tp)
    assert a_win == A_WIN and keep == A_WIN and ts == CHUNK and tp % DIFF_TQ == 0

    w_in_b, w_gate_b = w_in.astype(BF16), w_gate.astype(BF16)
    merge_w = tuple(x.astype(BF16) for x in (w_branch_a, w_branch_b, w_out, w_up, w_down, w_ple,
                                             w_ple_gate))
    gains = (g_mix_post, g_ffn_pre, g_ffn_post, g_ple)

    cache_a_kt = jnp.transpose(cache_a_k, (0, 1, 3, 4, 2)).reshape(depth, bs, A_WIDTH, a_win)
    cache_a_vt = jnp.transpose(cache_a_v, (0, 1, 3, 4, 2)).reshape(depth, bs, A_WIDTH, a_win)
    cache_b_kt = jnp.transpose(cache_b_k, (0, 1, 3, 4, 5, 2)).reshape(depth, bs, B_WIDTH, past)
    cache_b_vh = cache_b_v.reshape(depth, bs, past * B_HEADS, B_HEAD_WIDTH)
    ple_p = p_prompt.reshape(depth, bp * tp, -1)
    ple_s = p_sample.reshape(depth, bs * ts, -1)

    hp = x_prompt.reshape(bp * tp, d)
    hs = x_sample.reshape(bs * ts, d)
    tails = ([], [])
    stacked_p = stacked_s = None

    for i in range(depth):
        lam0 = _lambda_init(i)
        bias = _band_bias(rel_bias[i])

        akt, avt = _proj_tail(hp.reshape(bp, tp, d), g_mix_pre, w_in_b, i, keep)
        tails[0].append(akt)
        tails[1].append(avt)
        (qa, kab, vab, qb, kbb, vbt, gate), stacked_p = _proj(hp, g_mix_pre, w_in_b, w_gate_b, i,
                                                              ROW_TILE, tp, stacked_p)
        r3 = lambda a: a.reshape(bp, tp, a.shape[-1])
        oa = _band_prompt(r3(qa), r3(kab), r3(vab), bias, BAND_CHUNKS)
        ob = _diff_prompt(r3(qb), r3(kbb), vbt, diff_lambda, diff_subln, lam0, i, DIFF_HEADS, DIFF_TQ,
                          DIFF_TK)
        hp = _merge(hp, gate, oa.reshape(bp * tp, -1), ob.reshape(bp * tp, -1), ple_p, merge_w, gains,
                    i, MERGE_TILE, FF_CHUNK)

        (qa, kab, vab, qb, kbb, vbb, gate), stacked_s = _proj(hs, g_mix_pre, w_in_b, w_gate_b, i,
                                                              ROW_TILE, None, stacked_s)
        r3 = lambda a: a.reshape(bs, ts, a.shape[-1])
        oa = _band_sample(r3(qa), r3(kab), r3(vab), cache_a_kt, cache_a_vt, bias, i)
        ob = _diff_sample(r3(qb), r3(kbb), r3(vbb), cache_b_kt, cache_b_vh, diff_lambda, diff_subln,
                          lam0, i, DIFF_SAMPLE_TC)
        hs = _merge(hs, gate, oa.reshape(bs * ts, -1), ob.reshape(bs * ts, -1), ple_s, merge_w, gains,
                    i, MERGE_TILE, FF_CHUNK)

    kbt_p, vbh_p = stacked_p
    kaf_s, vaf_s, kbf_s, vbh_s = stacked_s
    a_feature_major = lambda a: jnp.transpose(
        jnp.stack(a).reshape(depth, bp, A_HEADS, HEAD_DIM, keep), (0, 1, 4, 2, 3))
    return (hp.reshape(bp, tp, d), hs.reshape(bs, ts, d),
            a_feature_major(tails[0]), a_feature_major(tails[1]),
            jnp.transpose(kbt_p.reshape(depth, bp, B_HEADS, 2, HEAD_DIM, tp), (0, 1, 5, 2, 3, 4)),
            vbh_p.reshape(depth, bp, tp, B_HEADS, B_HEAD_WIDTH),
            kaf_s.reshape(depth, bs, ts, A_HEADS, HEAD_DIM), vaf_s.reshape(depth, bs, ts, A_HEADS, HEAD_DIM),
            kbf_s.reshape(depth, bs, ts, B_HEADS, 2, HEAD_DIM),
            vbh_s.reshape(depth, bs, ts, B_HEADS, B_HEAD_WIDTH))
```

```python
import functools
import math

import jax
import jax.numpy as jnp
import numpy as np
from jax import lax
from jax.experimental import pallas as pl
from jax.experimental.pallas import tpu as pltpu

CHUNK = 64
HEAD_DIM = 64
A_HEADS = 8
A_BAND = 8
A_WIDTH = A_HEADS * HEAD_DIM
A_WIN = A_BAND * CHUNK
A_KEYS = A_WIN + CHUNK
REL_CLIP = 128
B_HEADS = 8
B_HEAD_WIDTH = 2 * HEAD_DIM
B_WIDTH = B_HEADS * B_HEAD_WIDTH
NORM_EPS = 1e-6
MASK_VALUE = -1e30
QK_SCALE = HEAD_DIM ** -0.5
LOG2E = math.log2(math.e)
SUM_ROWS = 16

STAGE_SKEW_SCORES = 1
STAGE_SKEW_PV = 0
BAND_SCORE_SKEW = 1
SAMPLE_SCORE_SKEW = B_HEADS

A_GROUP = 4
A_GROUP_WIDTH = A_GROUP * HEAD_DIM
LANES = 128

V7X_VMEM_LIMIT_BYTES = 60000 * 1024

BF16 = jnp.bfloat16
F32 = jnp.float32


def _lambda_init(layer):
    return 0.8 - 0.6 * math.exp(-0.3 * layer)


def _alibi_slopes(n):
    return jnp.asarray(2.0 ** (-8.0 * np.arange(1, n + 1) / n), dtype=F32)


def _rms(x, g):
    return x * lax.rsqrt(jnp.mean(x * x, axis=-1, keepdims=True) + NORM_EPS) * g


def _dot(a, b):
    return jnp.dot(a, b, preferred_element_type=F32)


def _dot_nt(a, b):
    return lax.dot_general(a, b, (((1,), (1,)), ((), ())), preferred_element_type=F32)


def _resident(shape):
    nd = len(shape)
    return pl.BlockSpec(shape, lambda *_: (0,) * nd, pipeline_mode=pl.Buffered(1))


def _layer(shape, layer):
    nd = len(shape)
    return pl.BlockSpec((None,) + tuple(shape), lambda *_: (layer,) + (0,) * nd,
                        pipeline_mode=pl.Buffered(1))


def _params(*sem):
    return pltpu.CompilerParams(dimension_semantics=sem, vmem_limit_bytes=V7X_VMEM_LIMIT_BYTES)


IN_QA, IN_KA, IN_VA = 0, A_WIDTH, 2 * A_WIDTH
IN_QB, IN_KB, IN_VB = 3 * A_WIDTH, 3 * A_WIDTH + B_WIDTH, 3 * A_WIDTH + 2 * B_WIDTH


def _store_head_tiled(o_ref, x):
    rows = x.shape[0]
    for h in range(B_HEADS):
        o_ref[pl.ds(h, rows, stride=B_HEADS), :] = x[:, h * B_HEAD_WIDTH:(h + 1) * B_HEAD_WIDTH]


def _proj_kernel(prompt, n_carried, layer, h_ref, g_ref, win_ref, wg_ref, *refs):
    out_refs = refs[n_carried:]

    def stacked(ref):
        if n_carried:
            return ref
        for other in range(ref.shape[0]):
            if other != layer:
                ref[other] = jnp.zeros(ref.shape[1:], ref.dtype)
        return ref.at[layer]

    u = _rms(h_ref[...], g_ref[...]).astype(BF16)

    def col(lo, width):
        return _dot(u, win_ref[:, lo:lo + width])

    qa = (col(IN_QA, A_WIDTH) * QK_SCALE).astype(BF16)
    ka = col(IN_KA, A_WIDTH)
    va = col(IN_VA, A_WIDTH)
    qb = (col(IN_QB, B_WIDTH) * (QK_SCALE * LOG2E)).astype(BF16)
    kb = col(IN_KB, B_WIDTH)
    vb = col(IN_VB, B_WIDTH)
    gate = jax.nn.sigmoid(_dot(u, wg_ref[...])).astype(BF16)
    if prompt:
        qa_ref, kab_ref, vab_ref, qb_ref, kbb_ref, kbt_ref, vbt_ref, vbh_ref, gate_ref = out_refs
        stacked(kbt_ref)[...] = kb.T
        vbt_ref[...] = vb.T.astype(BF16)
    else:
        (qa_ref, kab_ref, vab_ref, qb_ref, kbb_ref, kaf_ref, vaf_ref, kbf_ref, vbb_ref, vbh_ref,
         gate_ref) = out_refs
        stacked(kaf_ref)[...] = ka
        stacked(vaf_ref)[...] = va
        stacked(kbf_ref)[...] = kb
        vbb_ref[...] = vb.astype(BF16)
    qa_ref[...] = qa
    kab_ref[...] = ka.astype(BF16)
    vab_ref[...] = va.astype(BF16)
    qb_ref[...] = qb
    kbb_ref[...] = kb.astype(BF16)
    _store_head_tiled(stacked(vbh_ref), vb)
    gate_ref[...] = gate


def _proj(h, g, w_in, w_gate, layer, tm, prompt_seq, carried):
    n, d = h.shape
    depth, _, gate_w = w_gate.shape
    first = carried is None
    row = lambda w: pl.BlockSpec((tm, w), lambda i: (i, 0))
    sds = jax.ShapeDtypeStruct
    tok = lambda w, dt: (row(w), sds((n, w), dt))

    def layer_block(block, index_map):
        if first:
            return pl.BlockSpec((depth,) + block, lambda i: (0,) + index_map(i))
        return pl.BlockSpec((None,) + block, lambda i: (layer,) + index_map(i))

    tok_stacked = lambda w: (layer_block((tm, w), lambda i: (i, 0)), sds((depth, n, w), F32))
    head_tiled = (layer_block((tm * B_HEADS, B_HEAD_WIDTH), lambda i: (i, 0)),
                  sds((depth, n * B_HEADS, B_HEAD_WIDTH), F32))
    local = [tok(A_WIDTH, BF16), tok(A_WIDTH, BF16), tok(A_WIDTH, BF16), tok(B_WIDTH, BF16),
             tok(B_WIDTH, BF16)]
    if prompt_seq is not None:
        nblk = prompt_seq // tm
        b = n // prompt_seq
        outs = local + [
            (layer_block((None, B_WIDTH, tm), lambda i: (i // nblk, 0, i % nblk)),
             sds((depth, b, B_WIDTH, prompt_seq), F32)),
            (pl.BlockSpec((None, None, B_WIDTH, tm), lambda i: (i // nblk, i % nblk, 0, 0)),
             sds((b, nblk, B_WIDTH, tm), BF16)),
            head_tiled, tok(gate_w, BF16)]
        stacked_idx = [5, 7]
    else:
        outs = local + [tok_stacked(A_WIDTH), tok_stacked(A_WIDTH), tok_stacked(B_WIDTH),
                        tok(B_WIDTH, BF16), head_tiled, tok(gate_w, BF16)]
        stacked_idx = [5, 6, 7, 9]
    n_fixed_inputs = 4
    carried = [] if carried is None else list(carried)
    aliases = {n_fixed_inputs + k: stacked_idx[k] for k in range(len(carried))}
    res = pl.pallas_call(
        functools.partial(_proj_kernel, prompt_seq is not None, len(carried), layer),
        grid=(n // tm,),
        in_specs=[row(d), _layer((1, d), layer), _layer(w_in.shape[1:], layer),
                  _layer(w_gate.shape[1:], layer)] + [pl.BlockSpec(memory_space=pl.ANY)] * len(carried),
        out_specs=[s for s, _ in outs],
        out_shape=[s for _, s in outs],
        input_output_aliases=aliases,
        compiler_params=_params("arbitrary"),
        name="proj",
    )(h, g.reshape(g.shape[0], 1, d), w_in, w_gate, *carried)
    return ([r for k, r in enumerate(res) if k not in stacked_idx], [res[k] for k in stacked_idx])


def _proj_tail_kernel(h_ref, g_ref, wk_ref, wv_ref, kt_ref, vt_ref):
    u = _rms(h_ref[...], g_ref[...]).astype(BF16)
    kt_ref[...] = _dot(u, wk_ref[...]).T
    vt_ref[...] = _dot(u, wv_ref[...]).T


def _proj_tail(h3, g, w_in, layer, keep):
    b, t, d = h3.shape
    last = t // keep - 1
    wcol = lambda c: pl.BlockSpec((None, d, A_WIDTH), lambda i: (layer, 0, c), pipeline_mode=pl.Buffered(1))
    out = pl.BlockSpec((None, A_WIDTH, keep), lambda i: (i, 0, 0))
    return pl.pallas_call(
        _proj_tail_kernel,
        grid=(b,),
        in_specs=[pl.BlockSpec((None, keep, d), lambda i: (i, last, 0)), _layer((1, d), layer),
                  wcol(IN_KA // A_WIDTH), wcol(IN_VA // A_WIDTH)],
        out_specs=[out, out],
        out_shape=[jax.ShapeDtypeStruct((b, A_WIDTH, keep), F32)] * 2,
        compiler_params=_params("arbitrary"),
        name="proj_tail",
    )(h3, g.reshape(g.shape[0], 1, d), w_in, w_in)


def _band_bias_kernel(tab_ref, o_ref):
    qi = lax.broadcasted_iota(jnp.int32, (CHUNK, A_KEYS), 0)
    ki = lax.broadcasted_iota(jnp.int32, (CHUNK, A_KEYS), 1)
    idx = jnp.clip(A_WIN + qi - ki, -REL_CLIP, REL_CLIP) + REL_CLIP
    lo = REL_CLIP - CHUNK + 1
    hi = 2 * REL_CLIP

    for h in range(A_HEADS):
        g, hh = divmod(h, A_GROUP)
        o_ref[g, hh * CHUNK:(hh + 1) * CHUNK, :] = jnp.full((CHUNK, A_KEYS), tab_ref[h, hi], F32)

    def body(r, carry):
        hit = idx == r
        for h in range(A_HEADS):
            g, hh = divmod(h, A_GROUP)
            sl = (g, slice(hh * CHUNK, (hh + 1) * CHUNK), slice(None))
            o_ref[sl] = jnp.where(hit, tab_ref[h, r], o_ref[sl])
        return carry

    lax.fori_loop(lo, hi, body, 0)


def _band_bias(table):
    return pl.pallas_call(
        _band_bias_kernel,
        in_specs=[pl.BlockSpec(memory_space=pltpu.SMEM)],
        out_specs=pl.BlockSpec(memory_space=pltpu.VMEM),
        out_shape=jax.ShapeDtypeStruct((A_HEADS // A_GROUP, A_GROUP * CHUNK, A_KEYS), F32),
        name="band_bias",
    )(table)


def _block_diag_query(qg):
    lane_head = lax.broadcasted_iota(jnp.int32, qg.shape, 1) // HEAD_DIM
    return jnp.concatenate(
        [jnp.where(lane_head == hh, qg, jnp.zeros_like(qg)) for hh in range(A_GROUP)], axis=0)


def _gather_heads(o):
    lane_head = lax.broadcasted_iota(jnp.int32, (CHUNK, A_GROUP_WIDTH), 1) // HEAD_DIM
    og = jnp.zeros((CHUNK, A_GROUP_WIDTH), F32)
    for hh in range(A_GROUP):
        og = og + jnp.where(lane_head == hh, o[hh * CHUNK:(hh + 1) * CHUNK, :], 0.0)
    return og


def _band_prompt_kernel(nc, q_ref, k_ref, v_ref, bias_ref, o_ref, kpad_ref, vpad_ref):
    step = pl.program_id(1)
    t = k_ref.shape[0]

    @pl.when(step == 0)
    def _():
        kpad_ref[0:A_WIN, :] = jnp.zeros((A_WIN, A_WIDTH), BF16)
        vpad_ref[0:A_WIN, :] = jnp.zeros((A_WIN, A_WIDTH), BF16)
        kpad_ref[A_WIN:A_WIN + t, :] = k_ref[...]
        vpad_ref[A_WIN:A_WIN + t, :] = v_ref[...]

    chains = [(e, g) for e in range(nc) for g in range(A_HEADS // A_GROUP)]

    def window_start(e):
        return pl.multiple_of((step * nc + e) * CHUNK, CHUNK)

    def score(e, g):
        cols = slice(g * A_GROUP_WIDTH, (g + 1) * A_GROUP_WIDTH)
        q = q_ref[e * CHUNK:(e + 1) * CHUNK, cols]
        kw = kpad_ref[pl.ds(window_start(e), A_KEYS), cols]
        s = _dot_nt(_block_diag_query(q), kw) + bias_ref[g]
        kidx = lax.broadcasted_iota(jnp.int32, s.shape, 1)
        return jnp.where(kidx >= (A_BAND - (step * nc + e)) * CHUNK, s, MASK_VALUE)

    def finish(e, g, s):
        cols = slice(g * A_GROUP_WIDTH, (g + 1) * A_GROUP_WIDTH)
        m = jnp.max(s, axis=-1, keepdims=True)
        p = jnp.exp(s - m)
        l = jnp.sum(p, axis=-1, keepdims=True)
        vw = vpad_ref[pl.ds(window_start(e), A_KEYS), cols]
        o = _gather_heads(_dot(p.astype(BF16), vw) / l)
        o_ref[e * CHUNK:(e + 1) * CHUNK, cols] = o.astype(BF16)

    scores = {i: score(*chains[i]) for i in range(min(BAND_SCORE_SKEW, len(chains)))}
    for i, (e, g) in enumerate(chains):
        if i + BAND_SCORE_SKEW < len(chains):
            scores[i + BAND_SCORE_SKEW] = score(*chains[i + BAND_SCORE_SKEW])
        finish(e, g, scores.pop(i))


def _band_prompt(qa, ka, va, bias, nc):
    b, t, _ = qa.shape
    rows = nc * CHUNK
    return pl.pallas_call(
        functools.partial(_band_prompt_kernel, nc),
        grid=(b, t // rows),
        in_specs=[pl.BlockSpec((None, rows, A_WIDTH), lambda i, c: (i, c, 0)),
                  pl.BlockSpec((None, t, A_WIDTH), lambda i, c: (i, 0, 0)),
                  pl.BlockSpec((None, t, A_WIDTH), lambda i, c: (i, 0, 0)),
                  _resident(bias.shape)],
        out_specs=pl.BlockSpec((None, rows, A_WIDTH), lambda i, c: (i, c, 0)),
        out_shape=jax.ShapeDtypeStruct((b, t, A_WIDTH), BF16),
        scratch_shapes=[pltpu.VMEM((A_WIN + t, A_WIDTH), BF16),
                        pltpu.VMEM((A_WIN + t, A_WIDTH), BF16)],
        compiler_params=_params("arbitrary", "arbitrary"),
        name="band_prompt",
    )(qa, ka, va, bias)


def _band_sample_kernel(nb, q_ref, kct_ref, vct_ref, kn_ref, vn_ref, bias_ref, o_ref):
    chains = [(e, g) for e in range(nb) for g in range(A_HEADS // A_GROUP)]

    def score(e, g):
        cols = slice(g * A_GROUP_WIDTH, (g + 1) * A_GROUP_WIDTH)
        qbd = _block_diag_query(q_ref[e, :, cols])
        s_old = _dot(qbd, kct_ref[e, cols, :].astype(BF16)) + bias_ref[g, :, 0:A_WIN]
        s_new = _dot_nt(qbd, kn_ref[e, :, cols]) + bias_ref[g, :, A_WIN:A_KEYS]
        return s_old, s_new

    scores = [score(e, g) for e, g in chains]
    for (e, g), (s_old, s_new) in zip(chains, scores):
        cols = slice(g * A_GROUP_WIDTH, (g + 1) * A_GROUP_WIDTH)
        m = jnp.maximum(jnp.max(s_old, axis=-1, keepdims=True), jnp.max(s_new, axis=-1, keepdims=True))
        p_old = jnp.exp(s_old - m)
        p_new = jnp.exp(s_new - m)
        l = jnp.sum(p_old, axis=-1, keepdims=True) + jnp.sum(p_new, axis=-1, keepdims=True)
        o = (_dot_nt(p_old.astype(BF16), vct_ref[e, cols, :].astype(BF16))
             + _dot(p_new.astype(BF16), vn_ref[e, :, cols]))
        o_ref[e, :, cols] = _gather_heads(o / l).astype(BF16)


def _band_sample(qa, ka, va, cache_kt, cache_vt, bias, layer, nb):
    b, t, _ = qa.shape
    w = cache_kt.shape[3]
    new = pl.BlockSpec((nb, t, A_WIDTH), lambda i: (i, 0, 0))
    old = pl.BlockSpec((None, nb, A_WIDTH, w), lambda i: (layer, i, 0, 0))
    return pl.pallas_call(
        functools.partial(_band_sample_kernel, nb),
        grid=(b // nb,),
        in_specs=[new, old, old, new, new, _resident(bias.shape)],
        out_specs=new,
        out_shape=jax.ShapeDtypeStruct((b, t, A_WIDTH), BF16),
        compiler_params=_params("arbitrary"),
        name="band_sample",
    )(qa, cache_kt, cache_vt, ka, va, bias)


def _diff_lambda(lq_ref, lam0):
    lq = lq_ref[...]
    a = jnp.sum(lq[0:1, :] * lq[1:2, :], axis=-1, keepdims=True)
    b = jnp.sum(lq[2:3, :] * lq[3:4, :], axis=-1, keepdims=True)
    return jnp.exp(a) - jnp.exp(b) + lam0


def _stack_components(q):
    lane = lax.broadcasted_iota(jnp.int32, q.shape, 1)
    zero = jnp.zeros_like(q)
    return jnp.concatenate([jnp.where(lane < HEAD_DIM, q, zero),
                            jnp.where(lane >= HEAD_DIM, q, zero)], axis=0)


def _own_block_bias_and_mask(slope, qpos, kpos):
    bias = slope * jnp.minimum(kpos, 2 * qpos - kpos).astype(F32)
    valid = (kpos // CHUNK) <= (qpos // CHUNK)
    return bias, valid


def _diff_prompt_kernel(lam0, hb, tq, tk, slopes_ref, lq_ref, subg_ref, q_ref, k_ref, vt_ref, o_ref,
                        acc_ref):
    hg = pl.program_id(1)
    qi = pl.program_id(2)
    q_start = qi * tq
    tv = vt_ref.shape[2]
    lane_reps = 2 * tq // LANES
    key_iota = lax.broadcasted_iota(jnp.int32, (tk, LANES), 0)
    ones_rows = jnp.ones((SUM_ROWS, tv), BF16)
    head_cols = [slice(hh * B_HEAD_WIDTH, (hh + 1) * B_HEAD_WIDTH) for hh in range(hb)]
    slopes = [slopes_ref[hg * hb + hh] * LOG2E for hh in range(hb)]
    qs = [_stack_components(q_ref[:, c]) for c in head_cols]

    acc_ref[...] = jnp.zeros(acc_ref.shape, F32)

    def softmax_stage(s, m_old):
        m_new = jnp.maximum(m_old, jnp.max(s, axis=0, keepdims=True))
        return m_new, jnp.exp2(m_old - m_new), jnp.exp2(s - m_new).astype(BF16)

    def pv_stage(hh, first_v_block, alpha, pb):
        pv = None
        for u in range(tk // tv):
            vt = jnp.concatenate([vt_ref[first_v_block + u, head_cols[hh], :], ones_rows], axis=0)
            part = _dot(vt, pb[u * tv:(u + 1) * tv])
            pv = part if pv is None else pv + part
        acc_ref[hh] = alpha * acc_ref[hh] + pv

    def run_block(score, first_v_block, ms):
        scores, probs, out = {}, {}, []
        for step in range(hb + STAGE_SKEW_SCORES + STAGE_SKEW_PV):
            hs, he, hp = step, step - STAGE_SKEW_SCORES, step - STAGE_SKEW_SCORES - STAGE_SKEW_PV
            if hs < hb:
                scores[hs] = score(hs)
            if 0 <= he < hb:
                m_new, alpha, pb = softmax_stage(scores.pop(he), ms[he])
                out.append(m_new)
                probs[he] = (alpha, pb)
            if 0 <= hp < hb:
                pv_stage(hp, first_v_block, *probs.pop(hp))
        return tuple(out)

    def below(j, ms):
        ks = pl.multiple_of(j * tk, tk)

        def score(hh):
            kbias = slopes[hh] * (key_iota + (ks - q_start)).astype(F32)
            return (_dot_nt(k_ref[pl.ds(ks, tk), head_cols[hh]], qs[hh])
                    + jnp.tile(kbias, (1, lane_reps)))

        return run_block(score, j * (tk // tv), ms)

    m0 = jnp.full((1, 2 * tq), MASK_VALUE, F32)
    ms = lax.fori_loop(0, q_start // tk, below, (m0,) * hb)

    for d in range(tq // tk):
        ks = pl.multiple_of(q_start + d * tk, tk)
        kpos = lax.broadcasted_iota(jnp.int32, (tk, tq), 0) + d * tk
        qpos = lax.broadcasted_iota(jnp.int32, (tk, tq), 1)

        def score(hh):
            bias, valid = _own_block_bias_and_mask(slopes[hh], qpos, kpos)
            bias = jnp.concatenate([bias, bias], axis=1)
            valid = jnp.concatenate([valid, valid], axis=1)
            s = _dot_nt(k_ref[pl.ds(ks, tk), head_cols[hh]], qs[hh]) + bias
            return jnp.where(valid, s, MASK_VALUE)

        ms = run_block(score, (q_start + d * tk) // tv, ms)

    lam = _diff_lambda(lq_ref, lam0)
    for hh in range(hb):
        acc = acc_ref[hh]
        num = acc[0:B_HEAD_WIDTH]
        l = acc[B_HEAD_WIDTH:B_HEAD_WIDTH + 1]
        ot = num[:, :tq] / l[:, :tq] - lam * (num[:, tq:] / l[:, tq:])
        ot = ot * lax.rsqrt(jnp.mean(ot * ot, axis=0, keepdims=True) + NORM_EPS)
        o_ref[:, head_cols[hh]] = (ot.T * subg_ref[...] * (1.0 - lam0)).astype(BF16)


def _diff_prompt(qb, kb, vbt, lq, subg, lam0, layer, hb, tq, tk):
    b, t, _ = qb.shape
    nblk, tv = vbt.shape[1], vbt.shape[3]
    assert tk % tv == 0 and tq % tk == 0 and B_HEADS % hb == 0
    w = hb * B_HEAD_WIDTH
    blk = pl.BlockSpec((None, tq, w), lambda i, h, q: (i, q, h))
    return pl.pallas_call(
        functools.partial(_diff_prompt_kernel, lam0, hb, tq, tk),
        grid=(b, B_HEADS // hb, t // tq),
        in_specs=[pl.BlockSpec(memory_space=pltpu.SMEM), _layer(lq.shape[1:], layer),
                  _layer((1, B_HEAD_WIDTH), layer), blk,
                  pl.BlockSpec((None, t, w), lambda i, h, q: (i, 0, h)),
                  pl.BlockSpec((None, nblk, w, tv), lambda i, h, q: (i, 0, h, 0))],
        out_specs=blk,
        out_shape=jax.ShapeDtypeStruct((b, t, B_WIDTH), BF16),
        scratch_shapes=[pltpu.VMEM((hb, B_HEAD_WIDTH + SUM_ROWS, 2 * tq), F32)],
        compiler_params=_params("arbitrary", "arbitrary", "arbitrary"),
        name="diff_prompt",
    )(_alibi_slopes(B_HEADS), lq, subg.reshape(subg.shape[0], 1, B_HEAD_WIDTH), qb, kb, vbt)


def _diff_sample_kernel(lam0, nkc, slopes_ref, lq_ref, subg_ref, q_ref, kct_ref, vch_ref, kn_ref,
                        vn_ref, o_ref, m_ref, l_ref, acc_ref):
    kc = pl.program_id(1)
    t = q_ref.shape[0]
    tc = kct_ref.shape[1]
    past = tc * nkc

    @pl.when(kc == 0)
    def _():
        m_ref[...] = jnp.full(m_ref.shape, MASK_VALUE, F32)
        l_ref[...] = jnp.zeros(l_ref.shape, F32)
        acc_ref[...] = jnp.zeros(acc_ref.shape, F32)

    def update(h, s, v):
        width = s.shape[1]
        m_old = m_ref[h]
        m_new = jnp.maximum(m_old, jnp.max(s, axis=1, keepdims=True))
        alpha = jnp.exp2(m_old - m_new)
        if width % LANES == 0:
            p = jnp.exp2(s - jnp.tile(m_new, (1, width // LANES)))
        else:
            p = jnp.exp2(s - m_new[:, :width])
        v_and_ones = jnp.concatenate([v, jnp.ones(v.shape, BF16)], axis=1)
        pv = _dot(p.astype(BF16), v_and_ones)
        l_ref[h] = alpha * l_ref[h] + pv[:, B_HEAD_WIDTH:]
        acc_ref[h] = alpha * acc_ref[h] + pv[:, :B_HEAD_WIDTH]
        m_ref[h] = m_new

    kpos = (kc * tc - past + lax.broadcasted_iota(jnp.int32, (1, tc), 1)).astype(F32)
    def score(h):
        cols = slice(h * B_HEAD_WIDTH, (h + 1) * B_HEAD_WIDTH)
        qs = _stack_components(q_ref[:, cols])
        return _dot(qs, kct_ref[cols, :].astype(BF16)) + (slopes_ref[h] * LOG2E) * kpos

    scores = {h: score(h) for h in range(min(SAMPLE_SCORE_SKEW, B_HEADS))}
    for h in range(B_HEADS):
        if h + SAMPLE_SCORE_SKEW < B_HEADS:
            scores[h + SAMPLE_SCORE_SKEW] = score(h + SAMPLE_SCORE_SKEW)
        update(h, scores.pop(h), vch_ref[pl.ds(h, tc, stride=B_HEADS), :].astype(BF16))

    @pl.when(kc == nkc - 1)
    def _():
        lam = _diff_lambda(lq_ref, lam0)
        qpos = lax.broadcasted_iota(jnp.int32, (2 * t, t), 0) % t
        kidx = lax.broadcasted_iota(jnp.int32, (2 * t, t), 1)
        for h in range(B_HEADS):
            cols = slice(h * B_HEAD_WIDTH, (h + 1) * B_HEAD_WIDTH)
            qs = _stack_components(q_ref[:, cols])
            bias, _ = _own_block_bias_and_mask(slopes_ref[h] * LOG2E, qpos, kidx)
            update(h, _dot_nt(qs, kn_ref[:, cols]) + bias, vn_ref[:, cols])
            acc = acc_ref[h]
            l = l_ref[h]
            o = acc[:t] / l[:t] - lam * (acc[t:] / l[t:])
            o_ref[:, cols] = (_rms(o, subg_ref[...]) * (1.0 - lam0)).astype(BF16)


def _diff_sample(qb, kb, vb, cache_kt, cache_vh, lq, subg, lam0, layer, tc):
    b, t, _ = qb.shape
    past = cache_kt.shape[3]
    new = pl.BlockSpec((None, t, B_WIDTH), lambda i, k: (i, 0, 0))
    stat = pltpu.VMEM((B_HEADS, 2 * t, LANES), F32)
    return pl.pallas_call(
        functools.partial(_diff_sample_kernel, lam0, past // tc),
        grid=(b, past // tc),
        in_specs=[pl.BlockSpec(memory_space=pltpu.SMEM), _layer(lq.shape[1:], layer),
                  _layer((1, B_HEAD_WIDTH), layer), new,
                  pl.BlockSpec((None, None, B_WIDTH, tc), lambda i, k: (layer, i, 0, k)),
                  pl.BlockSpec((None, None, tc * B_HEADS, B_HEAD_WIDTH), lambda i, k: (layer, i, k, 0)),
                  new, new],
        out_specs=new,
        out_shape=jax.ShapeDtypeStruct((b, t, B_WIDTH), BF16),
        scratch_shapes=[stat, stat, pltpu.VMEM((B_HEADS, 2 * t, B_HEAD_WIDTH), F32)],
        compiler_params=_params("arbitrary", "arbitrary"),
        name="diff_sample",
    )(_alibi_slopes(B_HEADS), lq, subg.reshape(subg.shape[0], 1, B_HEAD_WIDTH), qb, cache_kt, cache_vh,
      kb, vb)


def _merge_kernel(ff_chunk, h_ref, gate_ref, oa_ref, ob_ref, ple_ref,
                  wba_ref, wbb_ref, wout_ref, wup_ref, wdown_ref, wple_ref, wpg_ref,
                  g_post_ref, g_fpre_ref, g_fpost_ref, g_ple_ref, o_ref):
    d = h_ref.shape[1]
    gate = gate_ref[...]
    mix = (gate[:, :d].astype(F32) * _dot(oa_ref[...], wba_ref[...])
           + gate[:, d:].astype(F32) * _dot(ob_ref[...], wbb_ref[...]))
    h = h_ref[...] + _rms(_dot(mix.astype(BF16), wout_ref[...]), g_post_ref[...])

    u = _rms(h, g_fpre_ref[...]).astype(BF16)
    f = jnp.zeros(h.shape, F32)
    for lo in range(0, wup_ref.shape[1], ff_chunk):
        act = jnp.square(jnp.maximum(_dot(u, wup_ref[:, lo:lo + ff_chunk]), 0.0))
        f = f + _dot(act.astype(BF16), wdown_ref[lo:lo + ff_chunk, :])
    h = h + _rms(f, g_fpost_ref[...])

    pg = jax.nn.sigmoid(_dot(h.astype(BF16), wpg_ref[...]))
    pe = _rms(_dot(ple_ref[...].astype(BF16), wple_ref[...]), g_ple_ref[...])
    o_ref[...] = h + pg * pe


def _merge(h, gate, oa, ob, ple, w, gains, layer, tm, ff_chunk):
    n, d = h.shape
    row = lambda a: pl.BlockSpec((tm, a.shape[1]), lambda i: (i, 0))
    acts = [h, gate, oa, ob]
    gains = [g.reshape(g.shape[0], 1, d) for g in gains]
    return pl.pallas_call(
        functools.partial(_merge_kernel, ff_chunk),
        grid=(n // tm,),
        in_specs=([row(a) for a in acts]
                  + [pl.BlockSpec((None, tm, ple.shape[2]), lambda i: (layer, i, 0))]
                  + [_layer(x.shape[1:], layer) for x in w] + [_layer((1, d), layer)] * 4),
        out_specs=pl.BlockSpec((tm, d), lambda i: (i, 0)),
        out_shape=jax.ShapeDtypeStruct((n, d), F32),
        compiler_params=_params("arbitrary"),
        name="merge",
    )(*acts, ple, *w, *gains)


ROW_TILE = 512
MERGE_TILE = 512
FF_CHUNK = 1024
BAND_CHUNKS = 8
BAND_SAMPLE_SEQS = 4
DIFF_HEADS = 4
DIFF_TQ = 512
DIFF_TK = 512
DIFF_SAMPLE_TC = 2048


def kernel(x_prompt, x_sample, cache_a_k, cache_a_v, cache_b_k, cache_b_v, p_prompt, p_sample,
           w_in, rel_bias, diff_lambda, diff_subln, w_gate, w_branch_a, w_branch_b, w_out,
           g_mix_pre, g_mix_post, g_ffn_pre, g_ffn_post, w_up, w_down, w_ple, g_ple, w_ple_gate):
    depth = w_in.shape[0]
    bp, tp, d = x_prompt.shape
    bs, ts, _ = x_sample.shape
    past = cache_b_k.shape[2]
    a_win = cache_a_k.shape[2]
    keep = min(A_WIN, tp)
    assert a_win == A_WIN and keep == A_WIN and ts == CHUNK and tp % DIFF_TQ == 0

    w_in_b, w_gate_b = w_in.astype(BF16), w_gate.astype(BF16)
    merge_w = tuple(x.astype(BF16) for x in (w_branch_a, w_branch_b, w_out, w_up, w_down, w_ple,
                                             w_ple_gate))
    gains = (g_mix_post, g_ffn_pre, g_ffn_post, g_ple)

    cache_a_kt = jnp.transpose(cache_a_k, (0, 1, 3, 4, 2)).reshape(depth, bs, A_WIDTH, a_win)
    cache_a_vt = jnp.transpose(cache_a_v, (0, 1, 3, 4, 2)).reshape(depth, bs, A_WIDTH, a_win)
    cache_b_kt = jnp.transpose(cache_b_k, (0, 1, 3, 4, 5, 2)).reshape(depth, bs, B_WIDTH, past)
    cache_b_vh = cache_b_v.reshape(depth, bs, past * B_HEADS, B_HEAD_WIDTH)
    ple_p = p_prompt.reshape(depth, bp * tp, -1)
    ple_s = p_sample.reshape(depth, bs * ts, -1)

    hp = x_prompt.reshape(bp * tp, d)
    hs = x_sample.reshape(bs * ts, d)
    tails = ([], [])
    stacked_p = stacked_s = None

    for i in range(depth):
        lam0 = _lambda_init(i)
        bias = _band_bias(rel_bias[i])

        akt, avt = _proj_tail(hp.reshape(bp, tp, d), g_mix_pre, w_in_b, i, keep)
        tails[0].append(akt)
        tails[1].append(avt)
        (qa, kab, vab, qb, kbb, vbt, gate), stacked_p = _proj(hp, g_mix_pre, w_in_b, w_gate_b, i,
                                                              ROW_TILE, tp, stacked_p)
        r3 = lambda a: a.reshape(bp, tp, a.shape[-1])
        oa = _band_prompt(r3(qa), r3(kab), r3(vab), bias, BAND_CHUNKS)
        ob = _diff_prompt(r3(qb), r3(kbb), vbt, diff_lambda, diff_subln, lam0, i, DIFF_HEADS, DIFF_TQ,
                          DIFF_TK)
        hp = _merge(hp, gate, oa.reshape(bp * tp, -1), ob.reshape(bp * tp, -1), ple_p, merge_w, gains,
                    i, MERGE_TILE, FF_CHUNK)

        (qa, kab, vab, qb, kbb, vbb, gate), stacked_s = _proj(hs, g_mix_pre, w_in_b, w_gate_b, i,
                                                              ROW_TILE, None, stacked_s)
        r3 = lambda a: a.reshape(bs, ts, a.shape[-1])
        oa = _band_sample(r3(qa), r3(kab), r3(vab), cache_a_kt, cache_a_vt, bias, i, BAND_SAMPLE_SEQS)
        ob = _diff_sample(r3(qb), r3(kbb), r3(vbb), cache_b_kt, cache_b_vh, diff_lambda, diff_subln,
                          lam0, i, DIFF_SAMPLE_TC)
        hs = _merge(hs, gate, oa.reshape(bs * ts, -1), ob.reshape(bs * ts, -1), ple_s, merge_w, gains,
                    i, MERGE_TILE, FF_CHUNK)

    kbt_p, vbh_p = stacked_p
    kaf_s, vaf_s, kbf_s, vbh_s = stacked_s
    a_feature_major = lambda a: jnp.transpose(
        jnp.stack(a).reshape(depth, bp, A_HEADS, HEAD_DIM, keep), (0, 1, 4, 2, 3))
    return (hp.reshape(bp, tp, d), hs.reshape(bs, ts, d),
            a_feature_major(tails[0]), a_feature_major(tails[1]),
            jnp.transpose(kbt_p.reshape(depth, bp, B_HEADS, 2, HEAD_DIM, tp), (0, 1, 5, 2, 3, 4)),
            vbh_p.reshape(depth, bp, tp, B_HEADS, B_HEAD_WIDTH),
            kaf_s.reshape(depth, bs, ts, A_HEADS, HEAD_DIM), vaf_s.reshape(depth, bs, ts, A_HEADS, HEAD_DIM),
            kbf_s.reshape(depth, bs, ts, B_HEADS, 2, HEAD_DIM),
            vbh_s.reshape(depth, bs, ts, B_HEADS, B_HEAD_WIDTH))
```

```python
import functools
import math

import jax
import jax.numpy as jnp
import numpy as np
from jax import lax
from jax.experimental import pallas as pl
from jax.experimental.pallas import tpu as pltpu

CHUNK = 64
HEAD_DIM = 64
A_HEADS = 8
A_BAND = 8
A_WIDTH = A_HEADS * HEAD_DIM
A_WIN = A_BAND * CHUNK
A_KEYS = A_WIN + CHUNK
REL_CLIP = 128
B_HEADS = 8
B_HEAD_WIDTH = 2 * HEAD_DIM
B_WIDTH = B_HEADS * B_HEAD_WIDTH
NORM_EPS = 1e-6
MASK_VALUE = -1e30
QK_SCALE = HEAD_DIM ** -0.5
LOG2E = math.log2(math.e)
SUM_ROWS = 16

STAGE_SKEW_SCORES = 1
STAGE_SKEW_PV = 0
BAND_SCORE_SKEW = 1
SAMPLE_SCORE_SKEW = B_HEADS
SAMPLE_RING = 3

A_GROUP = 4
A_GROUP_WIDTH = A_GROUP * HEAD_DIM
LANES = 128

V7X_VMEM_LIMIT_BYTES = 60000 * 1024

BF16 = jnp.bfloat16
F32 = jnp.float32


def _lambda_init(layer):
    return 0.8 - 0.6 * math.exp(-0.3 * layer)


def _alibi_slopes(n):
    return jnp.asarray(2.0 ** (-8.0 * np.arange(1, n + 1) / n), dtype=F32)


def _rms(x, g):
    return x * lax.rsqrt(jnp.mean(x * x, axis=-1, keepdims=True) + NORM_EPS) * g


def _dot(a, b):
    return jnp.dot(a, b, preferred_element_type=F32)


def _dot_nt(a, b):
    return lax.dot_general(a, b, (((1,), (1,)), ((), ())), preferred_element_type=F32)


def _resident(shape):
    nd = len(shape)
    return pl.BlockSpec(shape, lambda *_: (0,) * nd, pipeline_mode=pl.Buffered(1))


def _layer(shape, layer):
    nd = len(shape)
    return pl.BlockSpec((None,) + tuple(shape), lambda *_: (layer,) + (0,) * nd,
                        pipeline_mode=pl.Buffered(1))


def _params(*sem):
    return pltpu.CompilerParams(dimension_semantics=sem, vmem_limit_bytes=V7X_VMEM_LIMIT_BYTES)


IN_QA, IN_KA, IN_VA = 0, A_WIDTH, 2 * A_WIDTH
IN_QB, IN_KB, IN_VB = 3 * A_WIDTH, 3 * A_WIDTH + B_WIDTH, 3 * A_WIDTH + 2 * B_WIDTH


def _store_head_tiled(o_ref, x):
    rows = x.shape[0]
    for h in range(B_HEADS):
        o_ref[pl.ds(h, rows, stride=B_HEADS), :] = x[:, h * B_HEAD_WIDTH:(h + 1) * B_HEAD_WIDTH]


def _proj_kernel(prompt, n_carried, layer, h_ref, g_ref, win_ref, wg_ref, *refs):
    out_refs = refs[n_carried:]

    def stacked(ref):
        if n_carried:
            return ref
        for other in range(ref.shape[0]):
            if other != layer:
                ref[other] = jnp.zeros(ref.shape[1:], ref.dtype)
        return ref.at[layer]

    u = _rms(h_ref[...], g_ref[...]).astype(BF16)

    def col(lo, width):
        return _dot(u, win_ref[:, lo:lo + width])

    qa = (col(IN_QA, A_WIDTH) * QK_SCALE).astype(BF16)
    ka = col(IN_KA, A_WIDTH)
    va = col(IN_VA, A_WIDTH)
    qb = (col(IN_QB, B_WIDTH) * (QK_SCALE * LOG2E)).astype(BF16)
    kb = col(IN_KB, B_WIDTH)
    vb = col(IN_VB, B_WIDTH)
    gate = jax.nn.sigmoid(_dot(u, wg_ref[...])).astype(BF16)
    if prompt:
        qa_ref, kab_ref, vab_ref, qb_ref, kbb_ref, kbt_ref, vbt_ref, vbh_ref, gate_ref = out_refs
        stacked(kbt_ref)[...] = kb.T
        vbt_ref[...] = vb.T.astype(BF16)
    else:
        (qa_ref, kab_ref, vab_ref, qb_ref, kbb_ref, kaf_ref, vaf_ref, kbf_ref, vbb_ref, vbh_ref,
         gate_ref) = out_refs
        stacked(kaf_ref)[...] = ka
        stacked(vaf_ref)[...] = va
        stacked(kbf_ref)[...] = kb
        vbb_ref[...] = vb.astype(BF16)
    qa_ref[...] = qa
    kab_ref[...] = ka.astype(BF16)
    vab_ref[...] = va.astype(BF16)
    qb_ref[...] = qb
    kbb_ref[...] = kb.astype(BF16)
    _store_head_tiled(stacked(vbh_ref), vb)
    gate_ref[...] = gate


def _proj(h, g, w_in, w_gate, layer, tm, prompt_seq, carried):
    n, d = h.shape
    depth, _, gate_w = w_gate.shape
    first = carried is None
    row = lambda w: pl.BlockSpec((tm, w), lambda i: (i, 0))
    sds = jax.ShapeDtypeStruct
    tok = lambda w, dt: (row(w), sds((n, w), dt))

    def layer_block(block, index_map):
        if first:
            return pl.BlockSpec((depth,) + block, lambda i: (0,) + index_map(i))
        return pl.BlockSpec((None,) + block, lambda i: (layer,) + index_map(i))

    tok_stacked = lambda w: (layer_block((tm, w), lambda i: (i, 0)), sds((depth, n, w), F32))
    head_tiled = (layer_block((tm * B_HEADS, B_HEAD_WIDTH), lambda i: (i, 0)),
                  sds((depth, n * B_HEADS, B_HEAD_WIDTH), F32))
    local = [tok(A_WIDTH, BF16), tok(A_WIDTH, BF16), tok(A_WIDTH, BF16), tok(B_WIDTH, BF16),
             tok(B_WIDTH, BF16)]
    if prompt_seq is not None:
        nblk = prompt_seq // tm
        b = n // prompt_seq
        outs = local + [
            (layer_block((None, B_WIDTH, tm), lambda i: (i // nblk, 0, i % nblk)),
             sds((depth, b, B_WIDTH, prompt_seq), F32)),
            (pl.BlockSpec((None, None, B_WIDTH, tm), lambda i: (i // nblk, i % nblk, 0, 0)),
             sds((b, nblk, B_WIDTH, tm), BF16)),
            head_tiled, tok(gate_w, BF16)]
        stacked_idx = [5, 7]
    else:
        outs = local + [tok_stacked(A_WIDTH), tok_stacked(A_WIDTH), tok_stacked(B_WIDTH),
                        tok(B_WIDTH, BF16), head_tiled, tok(gate_w, BF16)]
        stacked_idx = [5, 6, 7, 9]
    n_fixed_inputs = 4
    carried = [] if carried is None else list(carried)
    aliases = {n_fixed_inputs + k: stacked_idx[k] for k in range(len(carried))}
    res = pl.pallas_call(
        functools.partial(_proj_kernel, prompt_seq is not None, len(carried), layer),
        grid=(n // tm,),
        in_specs=[row(d), _layer((1, d), layer), _layer(w_in.shape[1:], layer),
                  _layer(w_gate.shape[1:], layer)] + [pl.BlockSpec(memory_space=pl.ANY)] * len(carried),
        out_specs=[s for s, _ in outs],
        out_shape=[s for _, s in outs],
        input_output_aliases=aliases,
        compiler_params=_params("arbitrary"),
        name="proj",
    )(h, g.reshape(g.shape[0], 1, d), w_in, w_gate, *carried)
    return ([r for k, r in enumerate(res) if k not in stacked_idx], [res[k] for k in stacked_idx])


def _proj_tail_kernel(h_ref, g_ref, wk_ref, wv_ref, kt_ref, vt_ref):
    u = _rms(h_ref[...], g_ref[...]).astype(BF16)
    kt_ref[...] = _dot(u, wk_ref[...]).T
    vt_ref[...] = _dot(u, wv_ref[...]).T


def _proj_tail(h3, g, w_in, layer, keep):
    b, t, d = h3.shape
    last = t // keep - 1
    wcol = lambda c: pl.BlockSpec((None, d, A_WIDTH), lambda i: (layer, 0, c), pipeline_mode=pl.Buffered(1))
    out = pl.BlockSpec((None, A_WIDTH, keep), lambda i: (i, 0, 0))
    return pl.pallas_call(
        _proj_tail_kernel,
        grid=(b,),
        in_specs=[pl.BlockSpec((None, keep, d), lambda i: (i, last, 0)), _layer((1, d), layer),
                  wcol(IN_KA // A_WIDTH), wcol(IN_VA // A_WIDTH)],
        out_specs=[out, out],
        out_shape=[jax.ShapeDtypeStruct((b, A_WIDTH, keep), F32)] * 2,
        compiler_params=_params("arbitrary"),
        name="proj_tail",
    )(h3, g.reshape(g.shape[0], 1, d), w_in, w_in)


def _band_bias_kernel(tab_ref, o_ref):
    qi = lax.broadcasted_iota(jnp.int32, (CHUNK, A_KEYS), 0)
    ki = lax.broadcasted_iota(jnp.int32, (CHUNK, A_KEYS), 1)
    idx = jnp.clip(A_WIN + qi - ki, -REL_CLIP, REL_CLIP) + REL_CLIP
    lo = REL_CLIP - CHUNK + 1
    hi = 2 * REL_CLIP

    for h in range(A_HEADS):
        g, hh = divmod(h, A_GROUP)
        o_ref[g, hh * CHUNK:(hh + 1) * CHUNK, :] = jnp.full((CHUNK, A_KEYS), tab_ref[h, hi], F32)

    def body(r, carry):
        hit = idx == r
        for h in range(A_HEADS):
            g, hh = divmod(h, A_GROUP)
            sl = (g, slice(hh * CHUNK, (hh + 1) * CHUNK), slice(None))
            o_ref[sl] = jnp.where(hit, tab_ref[h, r], o_ref[sl])
        return carry

    lax.fori_loop(lo, hi, body, 0)


def _band_bias(table):
    return pl.pallas_call(
        _band_bias_kernel,
        in_specs=[pl.BlockSpec(memory_space=pltpu.SMEM)],
        out_specs=pl.BlockSpec(memory_space=pltpu.VMEM),
        out_shape=jax.ShapeDtypeStruct((A_HEADS // A_GROUP, A_GROUP * CHUNK, A_KEYS), F32),
        name="band_bias",
    )(table)


def _block_diag_query(qg):
    lane_head = lax.broadcasted_iota(jnp.int32, qg.shape, 1) // HEAD_DIM
    return jnp.concatenate(
        [jnp.where(lane_head == hh, qg, jnp.zeros_like(qg)) for hh in range(A_GROUP)], axis=0)


def _gather_heads(o):
    lane_head = lax.broadcasted_iota(jnp.int32, (CHUNK, A_GROUP_WIDTH), 1) // HEAD_DIM
    og = jnp.zeros((CHUNK, A_GROUP_WIDTH), F32)
    for hh in range(A_GROUP):
        og = og + jnp.where(lane_head == hh, o[hh * CHUNK:(hh + 1) * CHUNK, :], 0.0)
    return og


def _band_prompt_kernel(nc, q_ref, k_ref, v_ref, bias_ref, o_ref, kpad_ref, vpad_ref):
    step = pl.program_id(1)
    t = k_ref.shape[0]

    @pl.when(step == 0)
    def _():
        kpad_ref[0:A_WIN, :] = jnp.zeros((A_WIN, A_WIDTH), BF16)
        vpad_ref[0:A_WIN, :] = jnp.zeros((A_WIN, A_WIDTH), BF16)
        kpad_ref[A_WIN:A_WIN + t, :] = k_ref[...]
        vpad_ref[A_WIN:A_WIN + t, :] = v_ref[...]

    chains = [(e, g) for e in range(nc) for g in range(A_HEADS // A_GROUP)]

    def window_start(e):
        return pl.multiple_of((step * nc + e) * CHUNK, CHUNK)

    def score(e, g):
        cols = slice(g * A_GROUP_WIDTH, (g + 1) * A_GROUP_WIDTH)
        q = q_ref[e * CHUNK:(e + 1) * CHUNK, cols]
        kw = kpad_ref[pl.ds(window_start(e), A_KEYS), cols]
        s = _dot_nt(_block_diag_query(q), kw) + bias_ref[g]
        kidx = lax.broadcasted_iota(jnp.int32, s.shape, 1)
        return jnp.where(kidx >= (A_BAND - (step * nc + e)) * CHUNK, s, MASK_VALUE)

    def finish(e, g, s):
        cols = slice(g * A_GROUP_WIDTH, (g + 1) * A_GROUP_WIDTH)
        m = jnp.max(s, axis=-1, keepdims=True)
        p = jnp.exp(s - m)
        l = jnp.sum(p, axis=-1, keepdims=True)
        vw = vpad_ref[pl.ds(window_start(e), A_KEYS), cols]
        o = _gather_heads(_dot(p.astype(BF16), vw) / l)
        o_ref[e * CHUNK:(e + 1) * CHUNK, cols] = o.astype(BF16)

    scores = {i: score(*chains[i]) for i in range(min(BAND_SCORE_SKEW, len(chains)))}
    for i, (e, g) in enumerate(chains):
        if i + BAND_SCORE_SKEW < len(chains):
            scores[i + BAND_SCORE_SKEW] = score(*chains[i + BAND_SCORE_SKEW])
        finish(e, g, scores.pop(i))


def _band_prompt(qa, ka, va, bias, nc):
    b, t, _ = qa.shape
    rows = nc * CHUNK
    return pl.pallas_call(
        functools.partial(_band_prompt_kernel, nc),
        grid=(b, t // rows),
        in_specs=[pl.BlockSpec((None, rows, A_WIDTH), lambda i, c: (i, c, 0)),
                  pl.BlockSpec((None, t, A_WIDTH), lambda i, c: (i, 0, 0)),
                  pl.BlockSpec((None, t, A_WIDTH), lambda i, c: (i, 0, 0)),
                  _resident(bias.shape)],
        out_specs=pl.BlockSpec((None, rows, A_WIDTH), lambda i, c: (i, c, 0)),
        out_shape=jax.ShapeDtypeStruct((b, t, A_WIDTH), BF16),
        scratch_shapes=[pltpu.VMEM((A_WIN + t, A_WIDTH), BF16),
                        pltpu.VMEM((A_WIN + t, A_WIDTH), BF16)],
        compiler_params=_params("arbitrary", "arbitrary"),
        name="band_prompt",
    )(qa, ka, va, bias)


def _band_sample_kernel(nb, q_ref, kct_ref, vct_ref, kn_ref, vn_ref, bias_ref, o_ref):
    chains = [(e, g) for e in range(nb) for g in range(A_HEADS // A_GROUP)]

    def score(e, g):
        cols = slice(g * A_GROUP_WIDTH, (g + 1) * A_GROUP_WIDTH)
        qbd = _block_diag_query(q_ref[e, :, cols])
        s_old = _dot(qbd, kct_ref[e, cols, :].astype(BF16)) + bias_ref[g, :, 0:A_WIN]
        s_new = _dot_nt(qbd, kn_ref[e, :, cols]) + bias_ref[g, :, A_WIN:A_KEYS]
        return s_old, s_new

    scores = [score(e, g) for e, g in chains]
    for (e, g), (s_old, s_new) in zip(chains, scores):
        cols = slice(g * A_GROUP_WIDTH, (g + 1) * A_GROUP_WIDTH)
        m = jnp.maximum(jnp.max(s_old, axis=-1, keepdims=True), jnp.max(s_new, axis=-1, keepdims=True))
        p_old = jnp.exp(s_old - m)
        p_new = jnp.exp(s_new - m)
        l = jnp.sum(p_old, axis=-1, keepdims=True) + jnp.sum(p_new, axis=-1, keepdims=True)
        o = (_dot_nt(p_old.astype(BF16), vct_ref[e, cols, :].astype(BF16))
             + _dot(p_new.astype(BF16), vn_ref[e, :, cols]))
        o_ref[e, :, cols] = _gather_heads(o / l).astype(BF16)


def _band_sample(qa, ka, va, cache_kt, cache_vt, bias, layer, nb):
    b, t, _ = qa.shape
    w = cache_kt.shape[3]
    new = pl.BlockSpec((nb, t, A_WIDTH), lambda i: (i, 0, 0))
    old = pl.BlockSpec((None, nb, A_WIDTH, w), lambda i: (layer, i, 0, 0))
    return pl.pallas_call(
        functools.partial(_band_sample_kernel, nb),
        grid=(b // nb,),
        in_specs=[new, old, old, new, new, _resident(bias.shape)],
        out_specs=new,
        out_shape=jax.ShapeDtypeStruct((b, t, A_WIDTH), BF16),
        compiler_params=_params("arbitrary"),
        name="band_sample",
    )(qa, cache_kt, cache_vt, ka, va, bias)


def _diff_lambda(lq_ref, lam0):
    lq = lq_ref[...]
    a = jnp.sum(lq[0:1, :] * lq[1:2, :], axis=-1, keepdims=True)
    b = jnp.sum(lq[2:3, :] * lq[3:4, :], axis=-1, keepdims=True)
    return jnp.exp(a) - jnp.exp(b) + lam0


def _stack_components(q):
    lane = lax.broadcasted_iota(jnp.int32, q.shape, 1)
    zero = jnp.zeros_like(q)
    return jnp.concatenate([jnp.where(lane < HEAD_DIM, q, zero),
                            jnp.where(lane >= HEAD_DIM, q, zero)], axis=0)


def _own_block_bias_and_mask(slope, qpos, kpos):
    bias = slope * jnp.minimum(kpos, 2 * qpos - kpos).astype(F32)
    valid = (kpos // CHUNK) <= (qpos // CHUNK)
    return bias, valid


def _diff_prompt_kernel(lam0, hb, tq, tk, slopes_ref, lq_ref, subg_ref, q_ref, k_ref, vt_ref, o_ref,
                        acc_ref):
    hg = pl.program_id(1)
    qi = pl.program_id(2)
    q_start = qi * tq
    tv = vt_ref.shape[2]
    lane_reps = 2 * tq // LANES
    key_iota = lax.broadcasted_iota(jnp.int32, (tk, LANES), 0)
    ones_rows = jnp.ones((SUM_ROWS, tv), BF16)
    head_cols = [slice(hh * B_HEAD_WIDTH, (hh + 1) * B_HEAD_WIDTH) for hh in range(hb)]
    slopes = [slopes_ref[hg * hb + hh] * LOG2E for hh in range(hb)]
    qs = [_stack_components(q_ref[:, c]) for c in head_cols]

    acc_ref[...] = jnp.zeros(acc_ref.shape, F32)

    def softmax_stage(s, m_old):
        m_new = jnp.maximum(m_old, jnp.max(s, axis=0, keepdims=True))
        return m_new, jnp.exp2(m_old - m_new), jnp.exp2(s - m_new).astype(BF16)

    def pv_stage(hh, first_v_block, alpha, pb):
        pv = None
        for u in range(tk // tv):
            vt = jnp.concatenate([vt_ref[first_v_block + u, head_cols[hh], :], ones_rows], axis=0)
            part = _dot(vt, pb[u * tv:(u + 1) * tv])
            pv = part if pv is None else pv + part
        acc_ref[hh] = alpha * acc_ref[hh] + pv

    def run_block(score, first_v_block, ms):
        scores, probs, out = {}, {}, []
        for step in range(hb + STAGE_SKEW_SCORES + STAGE_SKEW_PV):
            hs, he, hp = step, step - STAGE_SKEW_SCORES, step - STAGE_SKEW_SCORES - STAGE_SKEW_PV
            if hs < hb:
                scores[hs] = score(hs)
            if 0 <= he < hb:
                m_new, alpha, pb = softmax_stage(scores.pop(he), ms[he])
                out.append(m_new)
                probs[he] = (alpha, pb)
            if 0 <= hp < hb:
                pv_stage(hp, first_v_block, *probs.pop(hp))
        return tuple(out)

    def below(j, ms):
        ks = pl.multiple_of(j * tk, tk)

        def score(hh):
            kbias = slopes[hh] * (key_iota + (ks - q_start)).astype(F32)
            return (_dot_nt(k_ref[pl.ds(ks, tk), head_cols[hh]], qs[hh])
                    + jnp.tile(kbias, (1, lane_reps)))

        return run_block(score, j * (tk // tv), ms)

    m0 = jnp.full((1, 2 * tq), MASK_VALUE, F32)
    ms = lax.fori_loop(0, q_start // tk, below, (m0,) * hb)

    for d in range(tq // tk):
        ks = pl.multiple_of(q_start + d * tk, tk)
        kpos = lax.broadcasted_iota(jnp.int32, (tk, tq), 0) + d * tk
        qpos = lax.broadcasted_iota(jnp.int32, (tk, tq), 1)

        def score(hh):
            bias, valid = _own_block_bias_and_mask(slopes[hh], qpos, kpos)
            bias = jnp.concatenate([bias, bias], axis=1)
            valid = jnp.concatenate([valid, valid], axis=1)
            s = _dot_nt(k_ref[pl.ds(ks, tk), head_cols[hh]], qs[hh]) + bias
            return jnp.where(valid, s, MASK_VALUE)

        ms = run_block(score, (q_start + d * tk) // tv, ms)

    lam = _diff_lambda(lq_ref, lam0)
    for hh in range(hb):
        acc = acc_ref[hh]
        num = acc[0:B_HEAD_WIDTH]
        l = acc[B_HEAD_WIDTH:B_HEAD_WIDTH + 1]
        ot = num[:, :tq] / l[:, :tq] - lam * (num[:, tq:] / l[:, tq:])
        ot = ot * lax.rsqrt(jnp.mean(ot * ot, axis=0, keepdims=True) + NORM_EPS)
        o_ref[:, head_cols[hh]] = (ot.T * subg_ref[...] * (1.0 - lam0)).astype(BF16)


def _diff_prompt(qb, kb, vbt, lq, subg, lam0, layer, hb, tq, tk):
    b, t, _ = qb.shape
    nblk, tv = vbt.shape[1], vbt.shape[3]
    assert tk % tv == 0 and tq % tk == 0 and B_HEADS % hb == 0
    w = hb * B_HEAD_WIDTH
    blk = pl.BlockSpec((None, tq, w), lambda i, h, q: (i, q, h))
    return pl.pallas_call(
        functools.partial(_diff_prompt_kernel, lam0, hb, tq, tk),
        grid=(b, B_HEADS // hb, t // tq),
        in_specs=[pl.BlockSpec(memory_space=pltpu.SMEM), _layer(lq.shape[1:], layer),
                  _layer((1, B_HEAD_WIDTH), layer), blk,
                  pl.BlockSpec((None, t, w), lambda i, h, q: (i, 0, h)),
                  pl.BlockSpec((None, nblk, w, tv), lambda i, h, q: (i, 0, h, 0))],
        out_specs=blk,
        out_shape=jax.ShapeDtypeStruct((b, t, B_WIDTH), BF16),
        scratch_shapes=[pltpu.VMEM((hb, B_HEAD_WIDTH + SUM_ROWS, 2 * tq), F32)],
        compiler_params=_params("arbitrary", "arbitrary", "arbitrary"),
        name="diff_prompt",
    )(_alibi_slopes(B_HEADS), lq, subg.reshape(subg.shape[0], 1, B_HEAD_WIDTH), qb, kb, vbt)


def _diff_sample_kernel(lam0, layer, nseq, nkc, slopes_ref, lq_ref, subg_ref, q_ref, kct_hbm, vch_hbm,
                        kn_ref, vn_ref, o_ref, m_ref, l_ref, acc_ref, kbuf_ref, vbuf_ref, ksem, vsem):
    kc = pl.program_id(1)
    t = q_ref.shape[0]
    tc = kbuf_ref.shape[2]
    past = tc * nkc
    step = pl.program_id(0) * nkc + kc
    total = nseq * nkc

    def chunk_copies(s, slot):
        seq, chunk = s // nkc, s % nkc
        k_src = kct_hbm.at[layer, seq, :, pl.ds(pl.multiple_of(chunk * tc, tc), tc)]
        v_rows = tc * B_HEADS
        v_src = vch_hbm.at[layer, seq, pl.ds(pl.multiple_of(chunk * v_rows, v_rows), v_rows), :]
        return (pltpu.make_async_copy(k_src, kbuf_ref.at[slot], ksem.at[slot]),
                pltpu.make_async_copy(v_src, vbuf_ref.at[slot], vsem.at[slot]))

    @pl.when(step == 0)
    def _():
        for s in range(min(SAMPLE_RING - 1, total)):
            for cp in chunk_copies(s, s):
                cp.start()

    ahead = step + (SAMPLE_RING - 1)

    @pl.when(ahead < total)
    def _():
        for cp in chunk_copies(ahead, ahead % SAMPLE_RING):
            cp.start()

    slot = step % SAMPLE_RING
    for cp in chunk_copies(step, slot):
        cp.wait()
    kct_ref = kbuf_ref.at[slot]
    vch_ref = vbuf_ref.at[slot]

    @pl.when(kc == 0)
    def _():
        m_ref[...] = jnp.full(m_ref.shape, MASK_VALUE, F32)
        l_ref[...] = jnp.zeros(l_ref.shape, F32)
        acc_ref[...] = jnp.zeros(acc_ref.shape, F32)

    def update(h, s, v):
        width = s.shape[1]
        m_old = m_ref[h]
        m_new = jnp.maximum(m_old, jnp.max(s, axis=1, keepdims=True))
        alpha = jnp.exp2(m_old - m_new)
        if width % LANES == 0:
            p = jnp.exp2(s - jnp.tile(m_new, (1, width // LANES)))
        else:
            p = jnp.exp2(s - m_new[:, :width])
        v_and_ones = jnp.concatenate([v, jnp.ones(v.shape, BF16)], axis=1)
        pv = _dot(p.astype(BF16), v_and_ones)
        l_ref[h] = alpha * l_ref[h] + pv[:, B_HEAD_WIDTH:]
        acc_ref[h] = alpha * acc_ref[h] + pv[:, :B_HEAD_WIDTH]
        m_ref[h] = m_new

    kpos = (kc * tc - past + lax.broadcasted_iota(jnp.int32, (1, tc), 1)).astype(F32)
    def score(h):
        cols = slice(h * B_HEAD_WIDTH, (h + 1) * B_HEAD_WIDTH)
        qs = _stack_components(q_ref[:, cols])
        return _dot(qs, kct_ref[cols, :].astype(BF16)) + (slopes_ref[h] * LOG2E) * kpos

    scores = {h: score(h) for h in range(min(SAMPLE_SCORE_SKEW, B_HEADS))}
    for h in range(B_HEADS):
        if h + SAMPLE_SCORE_SKEW < B_HEADS:
            scores[h + SAMPLE_SCORE_SKEW] = score(h + SAMPLE_SCORE_SKEW)
        update(h, scores.pop(h), vch_ref[pl.ds(h, tc, stride=B_HEADS), :].astype(BF16))

    @pl.when(kc == nkc - 1)
    def _():
        lam = _diff_lambda(lq_ref, lam0)
        qpos = lax.broadcasted_iota(jnp.int32, (2 * t, t), 0) % t
        kidx = lax.broadcasted_iota(jnp.int32, (2 * t, t), 1)
        for h in range(B_HEADS):
            cols = slice(h * B_HEAD_WIDTH, (h + 1) * B_HEAD_WIDTH)
            qs = _stack_components(q_ref[:, cols])
            bias, _ = _own_block_bias_and_mask(slopes_ref[h] * LOG2E, qpos, kidx)
            update(h, _dot_nt(qs, kn_ref[:, cols]) + bias, vn_ref[:, cols])
            acc = acc_ref[h]
            l = l_ref[h]
            o = acc[:t] / l[:t] - lam * (acc[t:] / l[t:])
            o_ref[:, cols] = (_rms(o, subg_ref[...]) * (1.0 - lam0)).astype(BF16)


def _diff_sample(qb, kb, vb, cache_kt, cache_vh, lq, subg, lam0, layer, tc):
    b, t, _ = qb.shape
    past = cache_kt.shape[3]
    new = pl.BlockSpec((None, t, B_WIDTH), lambda i, k: (i, 0, 0))
    stat = pltpu.VMEM((B_HEADS, 2 * t, LANES), F32)
    return pl.pallas_call(
        functools.partial(_diff_sample_kernel, lam0, layer, b, past // tc),
        grid=(b, past // tc),
        in_specs=[pl.BlockSpec(memory_space=pltpu.SMEM), _layer(lq.shape[1:], layer),
                  _layer((1, B_HEAD_WIDTH), layer), new,
                  pl.BlockSpec(memory_space=pl.ANY), pl.BlockSpec(memory_space=pl.ANY),
                  new, new],
        out_specs=new,
        out_shape=jax.ShapeDtypeStruct((b, t, B_WIDTH), BF16),
        scratch_shapes=[stat, stat, pltpu.VMEM((B_HEADS, 2 * t, B_HEAD_WIDTH), F32),
                        pltpu.VMEM((SAMPLE_RING, B_WIDTH, tc), F32),
                        pltpu.VMEM((SAMPLE_RING, tc * B_HEADS, B_HEAD_WIDTH), F32),
                        pltpu.SemaphoreType.DMA((SAMPLE_RING,)),
                        pltpu.SemaphoreType.DMA((SAMPLE_RING,))],
        compiler_params=_params("arbitrary", "arbitrary"),
        name="diff_sample",
    )(_alibi_slopes(B_HEADS), lq, subg.reshape(subg.shape[0], 1, B_HEAD_WIDTH), qb, cache_kt, cache_vh,
      kb, vb)


def _merge_kernel(ff_chunk, h_ref, gate_ref, oa_ref, ob_ref, ple_ref,
                  wba_ref, wbb_ref, wout_ref, wup_ref, wdown_ref, wple_ref, wpg_ref,
                  g_post_ref, g_fpre_ref, g_fpost_ref, g_ple_ref, o_ref):
    d = h_ref.shape[1]
    gate = gate_ref[...]
    mix = (gate[:, :d].astype(F32) * _dot(oa_ref[...], wba_ref[...])
           + gate[:, d:].astype(F32) * _dot(ob_ref[...], wbb_ref[...]))
    h = h_ref[...] + _rms(_dot(mix.astype(BF16), wout_ref[...]), g_post_ref[...])

    u = _rms(h, g_fpre_ref[...]).astype(BF16)
    f = jnp.zeros(h.shape, F32)
    for lo in range(0, wup_ref.shape[1], ff_chunk):
        act = jnp.square(jnp.maximum(_dot(u, wup_ref[:, lo:lo + ff_chunk]), 0.0))
        f = f + _dot(act.astype(BF16), wdown_ref[lo:lo + ff_chunk, :])
    h = h + _rms(f, g_fpost_ref[...])

    pg = jax.nn.sigmoid(_dot(h.astype(BF16), wpg_ref[...]))
    pe = _rms(_dot(ple_ref[...].astype(BF16), wple_ref[...]), g_ple_ref[...])
    o_ref[...] = h + pg * pe


def _merge(h, gate, oa, ob, ple, w, gains, layer, tm, ff_chunk):
    n, d = h.shape
    row = lambda a: pl.BlockSpec((tm, a.shape[1]), lambda i: (i, 0))
    acts = [h, gate, oa, ob]
    gains = [g.reshape(g.shape[0], 1, d) for g in gains]
    return pl.pallas_call(
        functools.partial(_merge_kernel, ff_chunk),
        grid=(n // tm,),
        in_specs=([row(a) for a in acts]
                  + [pl.BlockSpec((None, tm, ple.shape[2]), lambda i: (layer, i, 0))]
                  + [_layer(x.shape[1:], layer) for x in w] + [_layer((1, d), layer)] * 4),
        out_specs=pl.BlockSpec((tm, d), lambda i: (i, 0)),
        out_shape=jax.ShapeDtypeStruct((n, d), F32),
        compiler_params=_params("arbitrary"),
        name="merge",
    )(*acts, ple, *w, *gains)


ROW_TILE = 512
MERGE_TILE = 512
FF_CHUNK = 1024
BAND_CHUNKS = 8
BAND_SAMPLE_SEQS = 4
DIFF_HEADS = 4
DIFF_TQ = 512
DIFF_TK = 512
DIFF_SAMPLE_TC = 1024


def kernel(x_prompt, x_sample, cache_a_k, cache_a_v, cache_b_k, cache_b_v, p_prompt, p_sample,
           w_in, rel_bias, diff_lambda, diff_subln, w_gate, w_branch_a, w_branch_b, w_out,
           g_mix_pre, g_mix_post, g_ffn_pre, g_ffn_post, w_up, w_down, w_ple, g_ple, w_ple_gate):
    depth = w_in.shape[0]
    bp, tp, d = x_prompt.shape
    bs, ts, _ = x_sample.shape
    past = cache_b_k.shape[2]
    a_win = cache_a_k.shape[2]
    keep = min(A_WIN, tp)
    assert a_win == A_WIN and keep == A_WIN and ts == CHUNK and tp % DIFF_TQ == 0

    w_in_b, w_gate_b = w_in.astype(BF16), w_gate.astype(BF16)
    merge_w = tuple(x.astype(BF16) for x in (w_branch_a, w_branch_b, w_out, w_up, w_down, w_ple,
                                             w_ple_gate))
    gains = (g_mix_post, g_ffn_pre, g_ffn_post, g_ple)

    cache_a_kt = jnp.transpose(cache_a_k, (0, 1, 3, 4, 2)).reshape(depth, bs, A_WIDTH, a_win)
    cache_a_vt = jnp.transpose(cache_a_v, (0, 1, 3, 4, 2)).reshape(depth, bs, A_WIDTH, a_win)
    cache_b_kt = jnp.transpose(cache_b_k, (0, 1, 3, 4, 5, 2)).reshape(depth, bs, B_WIDTH, past)
    cache_b_vh = cache_b_v.reshape(depth, bs, past * B_HEADS, B_HEAD_WIDTH)
    ple_p = p_prompt.reshape(depth, bp * tp, -1)
    ple_s = p_sample.reshape(depth, bs * ts, -1)

    hp = x_prompt.reshape(bp * tp, d)
    hs = x_sample.reshape(bs * ts, d)
    tails = ([], [])
    stacked_p = stacked_s = None

    for i in range(depth):
        lam0 = _lambda_init(i)
        bias = _band_bias(rel_bias[i])

        akt, avt = _proj_tail(hp.reshape(bp, tp, d), g_mix_pre, w_in_b, i, keep)
        tails[0].append(akt)
        tails[1].append(avt)
        (qa, kab, vab, qb, kbb, vbt, gate), stacked_p = _proj(hp, g_mix_pre, w_in_b, w_gate_b, i,
                                                              ROW_TILE, tp, stacked_p)
        r3 = lambda a: a.reshape(bp, tp, a.shape[-1])
        oa = _band_prompt(r3(qa), r3(kab), r3(vab), bias, BAND_CHUNKS)
        ob = _diff_prompt(r3(qb), r3(kbb), vbt, diff_lambda, diff_subln, lam0, i, DIFF_HEADS, DIFF_TQ,
                          DIFF_TK)
        hp = _merge(hp, gate, oa.reshape(bp * tp, -1), ob.reshape(bp * tp, -1), ple_p, merge_w, gains,
                    i, MERGE_TILE, FF_CHUNK)

        (qa, kab, vab, qb, kbb, vbb, gate), stacked_s = _proj(hs, g_mix_pre, w_in_b, w_gate_b, i,
                                                              ROW_TILE, None, stacked_s)
        r3 = lambda a: a.reshape(bs, ts, a.shape[-1])
        oa = _band_sample(r3(qa), r3(kab), r3(vab), cache_a_kt, cache_a_vt, bias, i, BAND_SAMPLE_SEQS)
        ob = _diff_sample(r3(qb), r3(kbb), r3(vbb), cache_b_kt, cache_b_vh, diff_lambda, diff_subln,
                          lam0, i, DIFF_SAMPLE_TC)
        hs = _merge(hs, gate, oa.reshape(bs * ts, -1), ob.reshape(bs * ts, -1), ple_s, merge_w, gains,
                    i, MERGE_TILE, FF_CHUNK)

    kbt_p, vbh_p = stacked_p
    kaf_s, vaf_s, kbf_s, vbh_s = stacked_s
    a_feature_major = lambda a: jnp.transpose(
        jnp.stack(a).reshape(depth, bp, A_HEADS, HEAD_DIM, keep), (0, 1, 4, 2, 3))
    return (hp.reshape(bp, tp, d), hs.reshape(bs, ts, d),
            a_feature_major(tails[0]), a_feature_major(tails[1]),
            jnp.transpose(kbt_p.reshape(depth, bp, B_HEADS, 2, HEAD_DIM, tp), (0, 1, 5, 2, 3, 4)),
            vbh_p.reshape(depth, bp, tp, B_HEADS, B_HEAD_WIDTH),
            kaf_s.reshape(depth, bs, ts, A_HEADS, HEAD_DIM), vaf_s.reshape(depth, bs, ts, A_HEADS, HEAD_DIM),
            kbf_s.reshape(depth, bs, ts, B_HEADS, 2, HEAD_DIM),
            vbh_s.reshape(depth, bs, ts, B_HEADS, B_HEAD_WIDTH))
```
